```python
import jax, jax.numpy as jnp
from jax import lax
import numpy as np

D_MODEL = 2048
BATCH = 4
SEQ = 4096
DEPTH = 1
DEC_BATCH = 1
DEC_SEQ = 16384
PAST_LEN = 128

GRID_W = 64
HEAD_DIM = 128
A_HEADS = 8
A_KV_HEADS = 2
NA_HEADS = 8
NA_WIN_ROWS = 8
NA_WIN_COLS = 16
Q_BLOCK = 128
ROPE_THETA = 10000.0
N_GROUPS = 4
EXPERTS_PER_GROUP = 8
N_EXPERTS = N_GROUPS * EXPERTS_PER_GROUP
TOP_K = 2
D_EXPERT = D_MODEL // 4
MOE_BLOCK = 128
EPS = 1e-6
NEG_INF = -1e30

A_Q_W = A_HEADS * HEAD_DIM
A_KV_W = A_KV_HEADS * HEAD_DIM
NA_W = NA_HEADS * HEAD_DIM
IN_COLS = A_Q_W + 2 * A_KV_W + 3 * NA_W + 2 * D_MODEL
SPLIT_POINTS = [A_Q_W, A_Q_W + A_KV_W, A_Q_W + 2 * A_KV_W, A_Q_W + 2 * A_KV_W + NA_W,
                A_Q_W + 2 * A_KV_W + 2 * NA_W, A_Q_W + 2 * A_KV_W + 3 * NA_W]

kernel_name = "hybrid_gqa_natten_hiermoe_encoder"

f32 = jnp.float32


def rms_norm(x, g):
    x32 = x.astype(f32)
    y = x32 * lax.rsqrt(jnp.mean(x32 * x32, axis=-1, keepdims=True) + EPS)
    return (y * g.astype(f32)).astype(x.dtype)


def axial_rope_tables(seq_len):
    t = jnp.arange(seq_len, dtype=jnp.int32)
    pos = jnp.stack([t // GRID_W, t % GRID_W], axis=-1).astype(f32)
    half = HEAD_DIM // 2
    inv_freq = ROPE_THETA ** (-jnp.arange(0, half, 2, dtype=f32) / half)
    ang = pos[:, :, None] * inv_freq[None, None, :]
    ang = jnp.concatenate([ang, ang], axis=-1)
    return jnp.cos(ang), jnp.sin(ang)


def apply_axial_rope(x, cos, sin):
    B, S, H, _ = x.shape
    xa = x.astype(f32).reshape(B, S, H, 2, HEAD_DIM // 2)
    x1, x2 = jnp.split(xa, 2, axis=-1)
    rot = jnp.concatenate([-x2, x1], axis=-1)
    out = xa * cos[:, None] + rot * sin[:, None]
    return out.reshape(B, S, H, HEAD_DIM).astype(x.dtype)


def global_gqa(q, k, v):
    B, S, _, _ = q.shape
    G = A_HEADS // A_KV_HEADS
    nblk = S // Q_BLOCK
    qb = q.reshape(B, nblk, Q_BLOCK, A_KV_HEADS, G, HEAD_DIM).transpose(1, 0, 3, 4, 2, 5)
    kt = k.transpose(0, 2, 1, 3)
    vt = v.transpose(0, 2, 1, 3)
    scale = HEAD_DIM ** -0.5

    def block(qblk):
        s = jnp.einsum('bkgqd,bksd->bkgqs', qblk, kt, preferred_element_type=f32) * scale
        p = jax.nn.softmax(s, axis=-1).astype(vt.dtype)
        return jnp.einsum('bkgqs,bksd->bkgqd', p, vt)

    o = lax.map(block, qb)
    return o.transpose(1, 0, 4, 2, 3, 5).reshape(B, S, A_HEADS * HEAD_DIM)


def neighbourhood_attention(q, k, v, rpb):
    B, S, H, _ = q.shape
    rows = S // GRID_W
    wr = min(NA_WIN_ROWS, rows)
    band = min(wr + 1, rows)
    nblk = S // Q_BLOCK
    rows_per_blk = Q_BLOCK // GRID_W
    qb = q.reshape(B, nblk, Q_BLOCK, H, HEAD_DIM).transpose(1, 0, 3, 2, 4)
    kg = k.reshape(B, rows, GRID_W, H, HEAD_DIM).transpose(0, 3, 1, 2, 4)
    vg = v.reshape(B, rows, GRID_W, H, HEAD_DIM).transpose(0, 3, 1, 2, 4)
    n_keys = band * GRID_W
    qi = jnp.arange(Q_BLOCK, dtype=jnp.int32)
    q_dr, q_col = qi // GRID_W, qi % GRID_W
    kj = jnp.arange(n_keys, dtype=jnp.int32)
    k_dr, k_col = kj // GRID_W, kj % GRID_W
    col_start = jnp.clip(q_col - NA_WIN_COLS // 2, 0, GRID_W - NA_WIN_COLS)
    col_in = (k_col[None, :] >= col_start[:, None]) & (k_col[None, :] < col_start[:, None] + NA_WIN_COLS)
    dc = jnp.clip(k_col[None, :] - q_col[:, None] + NA_WIN_COLS - 1, 0, 2 * NA_WIN_COLS - 2)
    scale = HEAD_DIM ** -0.5

    def block(args):
        blk, qblk = args
        r0 = blk * rows_per_blk
        bs = jnp.clip(r0 - wr // 2, 0, rows - band)
        kb = lax.dynamic_slice_in_dim(kg, bs, band, axis=2).reshape(B, H, n_keys, HEAD_DIM)
        vb = lax.dynamic_slice_in_dim(vg, bs, band, axis=2).reshape(B, H, n_keys, HEAD_DIM)
        q_row = r0 + q_dr
        k_row = bs + k_dr
        row_start = jnp.clip(q_row - wr // 2, 0, rows - wr)
        row_in = (k_row[None, :] >= row_start[:, None]) & (k_row[None, :] < row_start[:, None] + wr)
        dr = jnp.clip(k_row[None, :] - q_row[:, None] + NA_WIN_ROWS - 1, 0, 2 * NA_WIN_ROWS - 2)
        bias = rpb[:, dr, dc].astype(f32)
        s = jnp.einsum('bhqd,bhkd->bhqk', qblk, kb, preferred_element_type=f32) * scale + bias[None]
        s = jnp.where((row_in & col_in)[None, None], s, NEG_INF)
        p = jax.nn.softmax(s, axis=-1).astype(vb.dtype)
        return jnp.einsum('bhqk,bhkd->bhqd', p, vb)

    o = lax.map(block, (jnp.arange(nblk, dtype=jnp.int32), qb))
    return o.transpose(1, 0, 3, 2, 4).reshape(B, S, H * HEAD_DIM)


def hier_moe(x, w_rg, b_rg, w_re, b_re, w_e_in, w_e_down):
    B, S, D = x.shape
    N = B * S
    xt = x.reshape(N, D)
    g_logits = jnp.einsum('nd,dg->ng', xt, w_rg, preferred_element_type=f32) + b_rg.astype(f32)
    g_prob = jax.nn.softmax(g_logits, axis=-1)
    g_sel = jnp.argmax(g_prob, axis=-1).astype(jnp.int32)
    g_w = jnp.max(g_prob, axis=-1)
    e_logits = jnp.einsum('nd,de->ne', xt, w_re, preferred_element_type=f32) + b_re.astype(f32)
    e_logits = e_logits.reshape(N, N_GROUPS, EXPERTS_PER_GROUP)
    e_logits = jnp.take_along_axis(e_logits, g_sel[:, None, None], axis=1)[:, 0]
    e_prob = jax.nn.softmax(e_logits, axis=-1)
    top_p, top_i = lax.top_k(e_prob, TOP_K)
    top_p = top_p / jnp.sum(top_p, axis=-1, keepdims=True)
    gate = g_w[:, None] * top_p
    expert = g_sel[:, None] * EXPERTS_PER_GROUP + top_i.astype(jnp.int32)

    e_flat = expert.reshape(-1)
    gate_flat = gate.reshape(-1)
    tok_flat = jnp.repeat(jnp.arange(N, dtype=jnp.int32), TOP_K)
    order = jnp.argsort(e_flat)
    e_sorted = e_flat[order]
    tok_sorted = tok_flat[order]
    gate_sorted = gate_flat[order]
    counts = jnp.bincount(e_flat, length=N_EXPERTS).astype(jnp.int32)
    padded = (counts + MOE_BLOCK - 1) // MOE_BLOCK * MOE_BLOCK
    pad_end = jnp.cumsum(padded)
    pad_start = pad_end - padded
    start = jnp.cumsum(counts) - counts
    rank = jnp.arange(N * TOP_K, dtype=jnp.int32) - start[e_sorted]
    dest = pad_start[e_sorted] + rank
    n_rows = N * TOP_K + N_EXPERTS * MOE_BLOCK
    nb = n_rows // MOE_BLOCK
    xbuf = jnp.zeros((n_rows, D), x.dtype).at[dest].set(xt[tok_sorted])
    blk_expert = jnp.clip(jnp.searchsorted(pad_end, jnp.arange(nb, dtype=jnp.int32) * MOE_BLOCK, side='right'),
                          0, N_EXPERTS - 1)

    def block(args):
        xb, e = args
        h = xb @ w_e_in[e]
        a, u = jnp.split(h, 2, axis=-1)
        return (jax.nn.silu(a) * u) @ w_e_down[e]

    ybuf = lax.map(block, (xbuf.reshape(nb, MOE_BLOCK, D), blk_expert)).reshape(n_rows, D)
    contrib = ybuf[dest].astype(f32) * gate_sorted[:, None]
    y = jnp.zeros((N, D), f32).at[tok_sorted].add(contrib)
    return y.astype(x.dtype).reshape(B, S, D)


def encoder_layer(x, norm1_g, w_in, q_norm_g, k_norm_g, na_rpb, w_branch_a, w_branch_b, w_out,
                  norm2_g, w_router_group, b_router_group, w_router_expert, b_router_expert,
                  w_exp_in, w_exp_down):
    B, S, D = x.shape
    h = rms_norm(x, norm1_g)
    proj = h @ w_in
    qa, ka, va, qn, kn, vn, gates = jnp.split(proj, SPLIT_POINTS, axis=-1)
    cos, sin = axial_rope_tables(S)
    qa = apply_axial_rope(rms_norm(qa.reshape(B, S, A_HEADS, HEAD_DIM), q_norm_g), cos, sin)
    ka = apply_axial_rope(rms_norm(ka.reshape(B, S, A_KV_HEADS, HEAD_DIM), k_norm_g), cos, sin)
    va = va.reshape(B, S, A_KV_HEADS, HEAD_DIM)
    o_a = global_gqa(qa, ka, va)
    o_n = neighbourhood_attention(qn.reshape(B, S, NA_HEADS, HEAD_DIM),
                                  kn.reshape(B, S, NA_HEADS, HEAD_DIM),
                                  vn.reshape(B, S, NA_HEADS, HEAD_DIM), na_rpb)
    g_a, g_b = jnp.split(jax.nn.sigmoid(gates.astype(f32)), 2, axis=-1)
    merged = (g_a * (o_a @ w_branch_a).astype(f32) + g_b * (o_n @ w_branch_b).astype(f32)).astype(x.dtype)
    x = x + merged @ w_out
    x = x + hier_moe(rms_norm(x, norm2_g), w_router_group, b_router_group, w_router_expert,
                     b_router_expert, w_exp_in, w_exp_down)
    return x


def setup_inputs(seed: int = 0) -> dict:
    key = jax.random.key(seed)
    ks = jax.random.split(key, 20)
    nrm = jax.random.normal
    L = DEPTH
    return {
        "x_prompt": nrm(ks[0], (BATCH, SEQ, D_MODEL), f32),
        "x_sample": nrm(ks[1], (DEC_BATCH, DEC_SEQ, D_MODEL), f32),
        "norm1_g": 1.0 + 0.01 * nrm(ks[2], (L, D_MODEL), f32),
        "w_in": nrm(ks[3], (L, D_MODEL, IN_COLS), f32) * D_MODEL ** -0.5,
        "q_norm_g": 1.0 + 0.01 * nrm(ks[4], (L, HEAD_DIM), f32),
        "k_norm_g": 1.0 + 0.01 * nrm(ks[5], (L, HEAD_DIM), f32),
        "na_rpb": 0.02 * nrm(ks[6], (L, NA_HEADS, 2 * NA_WIN_ROWS - 1, 2 * NA_WIN_COLS - 1), f32),
        "w_branch_a": nrm(ks[7], (L, A_Q_W, D_MODEL), f32) * A_Q_W ** -0.5,
        "w_branch_b": nrm(ks[8], (L, NA_W, D_MODEL), f32) * NA_W ** -0.5,
        "w_out": nrm(ks[9], (L, D_MODEL, D_MODEL), f32) * D_MODEL ** -0.5,
        "norm2_g": 1.0 + 0.01 * nrm(ks[10], (L, D_MODEL), f32),
        "w_router_group": nrm(ks[11], (L, D_MODEL, N_GROUPS), f32) * D_MODEL ** -0.5,
        "b_router_group": 0.01 * nrm(ks[12], (L, N_GROUPS), f32),
        "w_router_expert": nrm(ks[13], (L, D_MODEL, N_EXPERTS), f32) * D_MODEL ** -0.5,
        "b_router_expert": 0.01 * nrm(ks[14], (L, N_EXPERTS), f32),
        "w_exp_in": nrm(ks[15], (L, N_EXPERTS, D_MODEL, 2 * D_EXPERT), f32) * D_MODEL ** -0.5,
        "w_exp_down": nrm(ks[16], (L, N_EXPERTS, D_EXPERT, D_MODEL), f32) * D_EXPERT ** -0.5,
        "norm_f_g": 1.0 + 0.01 * nrm(ks[17], (D_MODEL,), f32),
    }


def reference(x_prompt, x_sample, norm1_g, w_in, q_norm_g, k_norm_g, na_rpb, w_branch_a, w_branch_b,
              w_out, norm2_g, w_router_group, b_router_group, w_router_expert, b_router_expert,
              w_exp_in, w_exp_down, norm_f_g):
    y_prompt = x_prompt
    y_sample = x_sample
    for l in range(DEPTH):
        layer_params = (norm1_g[l], w_in[l], q_norm_g[l], k_norm_g[l], na_rpb[l], w_branch_a[l],
                        w_branch_b[l], w_out[l], norm2_g[l], w_router_group[l], b_router_group[l],
                        w_router_expert[l], b_router_expert[l], w_exp_in[l], w_exp_down[l])
        y_prompt = encoder_layer(y_prompt, *layer_params)
        y_sample = encoder_layer(y_sample, *layer_params)
    y_prompt = rms_norm(y_prompt, norm_f_g)
    y_sample = rms_norm(y_sample, norm_f_g)
    return (y_prompt, y_sample)
```

```python
import functools

import jax
import jax.numpy as jnp
from jax import lax
from jax.experimental import pallas as pl
from jax.experimental.pallas import tpu as pltpu

f32 = jnp.float32
bf16 = jnp.bfloat16
i32 = jnp.int32

D_MODEL = 2048
GRID_W = 64
HEAD_DIM = 128
A_HEADS = 8
A_KV_HEADS = 2
A_GROUP = A_HEADS // A_KV_HEADS
NA_HEADS = 8
NA_WIN_ROWS = 8
NA_WIN_COLS = 16
Q_BLOCK = 128
ROPE_THETA = 10000.0
N_GROUPS = 4
EXPERTS_PER_GROUP = 8
N_EXPERTS = N_GROUPS * EXPERTS_PER_GROUP
D_EXPERT = D_MODEL // 4
EPS = 1e-6
NEG_INF = -1e30
SCALE = HEAD_DIM ** -0.5

QA0, KA0, VA0, QN0, KN0, VN0, GATE0 = 0, 8, 10, 12, 20, 28, 36
IN_COLS = (GATE0 + 2 * D_MODEL // HEAD_DIM) * HEAD_DIM

LANES = 128
NA_BAND_ROWS = 10
NA_KEYS = NA_BAND_ROWS * GRID_W
ROUTE_LANE0 = 4
EXPERT_BLOCK = 256
VMEM_LIMIT = 56 * 1024 * 1024


def _cparams(sem):
    return pltpu.CompilerParams(dimension_semantics=sem, vmem_limit_bytes=VMEM_LIMIT)


def _inproj_body(x_ref, g_ref, w_ref, o_ref, h_ref):
    @pl.when(pl.program_id(1) == 0)
    def _():
        x = x_ref[...]
        ms = jnp.mean(x * x, axis=-1, keepdims=True)
        h_ref[...] = (x * lax.rsqrt(ms + EPS) * g_ref[...]).astype(bf16)

    o_ref[...] = jnp.dot(h_ref[...], w_ref[...], preferred_element_type=f32).astype(o_ref.dtype)


def _inproj(x, g, w, tm=1024, tn=512):
    n, d = x.shape
    c = w.shape[1]
    tm = min(tm, n)
    return pl.pallas_call(
        _inproj_body,
        grid=(n // tm, c // tn),
        in_specs=[pl.BlockSpec((tm, d), lambda i, j: (i, 0)),
                  pl.BlockSpec((1, d), lambda i, j: (0, 0)),
                  pl.BlockSpec((d, tn), lambda i, j: (0, j))],
        out_specs=pl.BlockSpec((tm, tn), lambda i, j: (i, j)),
        out_shape=jax.ShapeDtypeStruct((n, c), bf16),
        scratch_shapes=[pltpu.VMEM((tm, d), bf16)],
        compiler_params=_cparams(("parallel", "arbitrary")),
        name="inproj",
    )(x, g, w)


def _qkprep_body(q_ref, k_ref, cos_ref, sin_ref, qg_ref, kg_ref, qo_ref, ko_ref):
    cos = cos_ref[...]
    sin = sin_ref[...]
    lane = lax.broadcasted_iota(i32, cos.shape, 1)
    first = (lane % (HEAD_DIM // 2)) < (HEAD_DIM // 4)

    def prep(x, g, scale):
        x = x.astype(f32)
        y = x * lax.rsqrt(jnp.mean(x * x, axis=-1, keepdims=True) + EPS) * g
        rot = jnp.where(first, pltpu.roll(y, HEAD_DIM - HEAD_DIM // 4, 1), pltpu.roll(y, HEAD_DIM // 4, 1))
        return (y * cos + rot * sin) * scale

    for h in range(A_HEADS):
        sl = slice(h * HEAD_DIM, (h + 1) * HEAD_DIM)
        qo_ref[:, sl] = prep(q_ref[:, sl], qg_ref[...], SCALE).astype(bf16)
    for h in range(A_KV_HEADS):
        sl = slice(h * HEAD_DIM, (h + 1) * HEAD_DIM)
        ko_ref[:, sl] = prep(k_ref[:, sl], kg_ref[...], 1.0).astype(bf16)


def _qkprep(proj, cos, sin, qg, kg, seq, tm=512):
    n = proj.shape[0]
    tm = min(tm, seq)
    spb = seq // tm
    qw, kw = A_HEADS * HEAD_DIM, A_KV_HEADS * HEAD_DIM
    return pl.pallas_call(
        _qkprep_body,
        grid=(n // tm,),
        in_specs=[pl.BlockSpec((tm, qw), lambda i: (i, QA0 * HEAD_DIM // qw)),
                  pl.BlockSpec((tm, kw), lambda i: (i, KA0 * HEAD_DIM // kw)),
                  pl.BlockSpec((tm, HEAD_DIM), lambda i: (i % spb, 0)),
                  pl.BlockSpec((tm, HEAD_DIM), lambda i: (i % spb, 0)),
                  pl.BlockSpec((1, HEAD_DIM), lambda i: (0, 0)),
                  pl.BlockSpec((1, HEAD_DIM), lambda i: (0, 0))],
        out_specs=[pl.BlockSpec((tm, qw), lambda i: (i, 0)),
                   pl.BlockSpec((tm, kw), lambda i: (i, 0))],
        out_shape=[jax.ShapeDtypeStruct((n, qw), bf16), jax.ShapeDtypeStruct((n, kw), bf16)],
        compiler_params=_cparams(("parallel",)),
        name="qkprep",
    )(proj, proj, cos, sin, qg, kg)


def _gqa_body(q_ref, k_ref, v_ref, o_ref, *, tk):
    tq = q_ref.shape[0]
    nkv = k_ref.shape[0] // tk
    for g in range(A_GROUP):
        sl = slice(g * HEAD_DIM, (g + 1) * HEAD_DIM)
        q = q_ref[:, sl]

        def step(j, carry):
            m, l, acc = carry
            off = pl.multiple_of(j * tk, tk)
            k = k_ref[pl.ds(off, tk), :]
            v = v_ref[pl.ds(off, tk), :]
            s = lax.dot_general(q, k, (((1,), (1,)), ((), ())), preferred_element_type=f32)
            m_new = jnp.maximum(m, jnp.max(s, axis=-1, keepdims=True))
            alpha = jnp.exp(m - m_new)
            p = jnp.exp(s - m_new)
            l = alpha * l + jnp.sum(p, axis=-1, keepdims=True)
            acc = alpha * acc + jnp.dot(p.astype(bf16), v, preferred_element_type=f32)
            return m_new, l, acc

        init = (jnp.full((tq, 1), NEG_INF, f32), jnp.zeros((tq, 1), f32), jnp.zeros((tq, HEAD_DIM), f32))
        _, l, acc = lax.fori_loop(0, nkv, step, init)
        o_ref[:, sl] = (acc / l).astype(o_ref.dtype)


def _gqa(q, k, proj, batch, seq, tq=512, tk=512):
    n = q.shape[0]
    tq, tk = min(tq, seq), min(tk, seq)
    nq = seq // tq
    gw = A_GROUP * HEAD_DIM
    return pl.pallas_call(
        functools.partial(_gqa_body, tk=tk),
        grid=(batch, A_KV_HEADS, nq),
        in_specs=[pl.BlockSpec((tq, gw), lambda b, h, i: (b * nq + i, h)),
                  pl.BlockSpec((seq, HEAD_DIM), lambda b, h, i: (b, h)),
                  pl.BlockSpec((seq, HEAD_DIM), lambda b, h, i: (b, VA0 + h))],
        out_specs=pl.BlockSpec((tq, gw), lambda b, h, i: (b * nq + i, h)),
        out_shape=jax.ShapeDtypeStruct((n, A_HEADS * HEAD_DIM), bf16),
        compiler_params=_cparams(("parallel", "parallel", "arbitrary")),
        name="gqa",
    )(q, k, proj)


def _na_bias_tables(rpb, rows):
    nblk = rows * GRID_W // Q_BLOCK
    rpq = Q_BLOCK // GRID_W
    blk = jnp.array([0, 1, 2, nblk - 2, nblk - 1], i32)
    qi = jnp.arange(Q_BLOCK, dtype=i32)
    q_dr, q_col = qi // GRID_W, qi % GRID_W
    kj = jnp.arange(NA_KEYS, dtype=i32)
    k_dr, k_col = kj // GRID_W, kj % GRID_W
    col_start = jnp.clip(q_col - NA_WIN_COLS // 2, 0, GRID_W - NA_WIN_COLS)
    col_in = (k_col[None, :] >= col_start[:, None]) & (k_col[None, :] < col_start[:, None] + NA_WIN_COLS)
    dc = jnp.clip(k_col[None, :] - q_col[:, None] + NA_WIN_COLS - 1, 0, 2 * NA_WIN_COLS - 2)
    r0 = blk * rpq
    bs = jnp.clip(r0 - NA_WIN_ROWS // 2, 0, rows - NA_BAND_ROWS)
    q_row = r0[:, None] + q_dr[None, :]
    k_row = bs[:, None] + k_dr[None, :]
    row_start = jnp.clip(q_row - NA_WIN_ROWS // 2, 0, rows - NA_WIN_ROWS)
    row_in = (k_row[:, None, :] >= row_start[:, :, None]) & (k_row[:, None, :] < row_start[:, :, None] + NA_WIN_ROWS)
    dr = jnp.clip(k_row[:, None, :] - q_row[:, :, None] + NA_WIN_ROWS - 1, 0, 2 * NA_WIN_ROWS - 2)
    bias = rpb.astype(f32)[:, dr, dc[None]]
    bias = jnp.where((row_in & col_in[None])[None], bias, NEG_INF)
    return bias.transpose(1, 0, 2, 3)


def _na_body(q_ref, k_ref, v_ref, tab_ref, o_ref):
    for h in range(q_ref.shape[1] // HEAD_DIM):
        sl = slice(h * HEAD_DIM, (h + 1) * HEAD_DIM)
        s = lax.dot_general(q_ref[:, sl], k_ref[:, sl], (((1,), (1,)), ((), ())), preferred_element_type=f32)
        s = s * SCALE + tab_ref[0, h]
        m = jnp.max(s, axis=-1, keepdims=True)
        p = jnp.exp(s - m)
        l = jnp.sum(p, axis=-1, keepdims=True)
        o = jnp.dot(p.astype(bf16), v_ref[:, sl], preferred_element_type=f32)
        o_ref[:, sl] = (o / l).astype(o_ref.dtype)


def _na(proj, tab, batch, seq, hg=4):
    n = proj.shape[0]
    nblk = seq // Q_BLOCK
    rows = seq // GRID_W
    rpq = Q_BLOCK // GRID_W
    w = hg * HEAD_DIM
    ngrp = NA_HEADS // hg

    def band(b, i):
        first_row = jnp.clip(i * rpq - NA_WIN_ROWS // 2, 0, rows - NA_BAND_ROWS)
        return pl.multiple_of(b * seq + first_row * GRID_W, GRID_W)

    def cls(i):
        return jnp.where(i < 2, i, jnp.where(i <= nblk - 3, 2, i - (nblk - 5)))

    return pl.pallas_call(
        _na_body,
        grid=(batch, nblk, ngrp),
        in_specs=[pl.BlockSpec((Q_BLOCK, w), lambda b, i, g: (b * nblk + i, QN0 // hg + g)),
                  pl.BlockSpec((pl.Element(NA_KEYS), pl.Element(w)),
                               lambda b, i, g: (band(b, i), (KN0 // hg + g) * w)),
                  pl.BlockSpec((pl.Element(NA_KEYS), pl.Element(w)),
                               lambda b, i, g: (band(b, i), (VN0 // hg + g) * w)),
                  pl.BlockSpec((1, hg, Q_BLOCK, NA_KEYS), lambda b, i, g: (cls(i), g, 0, 0))],
        out_specs=pl.BlockSpec((Q_BLOCK, w), lambda b, i, g: (b * nblk + i, g)),
        out_shape=jax.ShapeDtypeStruct((n, NA_HEADS * HEAD_DIM), bf16),
        compiler_params=_cparams(("parallel", "parallel", "arbitrary")),
        name="natten",
    )(proj, proj, proj, tab)


def _merge_body(oa_ref, on_ref, ga_ref, gb_ref, x_ref, wa_ref, wb_ref, wo_ref, g2_ref, wrh_ref, wrl_ref, br_ref,
                x1_ref, h2_ref, ri_ref, rf_ref, cnt_ref, carry_ref):
    tm = x_ref.shape[0]

    @pl.when(pl.program_id(0) == 0)
    def _():
        carry_ref[...] = jnp.zeros_like(carry_ref)

    a = jnp.dot(oa_ref[...], wa_ref[...], preferred_element_type=f32)
    b = jnp.dot(on_ref[...], wb_ref[...], preferred_element_type=f32)
    ga = jax.nn.sigmoid(ga_ref[...].astype(f32))
    gb = jax.nn.sigmoid(gb_ref[...].astype(f32))
    merged = (ga * a + gb * b).astype(bf16)
    x1 = x_ref[...] + jnp.dot(merged, wo_ref[...], preferred_element_type=f32)
    x1_ref[...] = x1

    h2 = x1 * lax.rsqrt(jnp.mean(x1 * x1, axis=-1, keepdims=True) + EPS) * g2_ref[...]
    h2_ref[...] = h2
    h_hi = h2.astype(bf16)
    h_lo = (h2 - h_hi.astype(f32)).astype(bf16)
    logits = (jnp.dot(h_hi, wrh_ref[...], preferred_element_type=f32)
              + jnp.dot(h_lo, wrh_ref[...], preferred_element_type=f32)
              + jnp.dot(h_hi, wrl_ref[...], preferred_element_type=f32)) + br_ref[...]

    lane = lax.broadcasted_iota(i32, logits.shape, 1)
    gl = jnp.where(lane < N_GROUPS, logits, NEG_INF)
    gmax = jnp.max(gl, axis=-1, keepdims=True)
    gsel = jnp.min(jnp.where(gl == gmax, lane, LANES), axis=-1, keepdims=True)
    gw = 1.0 / jnp.sum(jnp.exp(gl - gmax), axis=-1, keepdims=True)
    lo = ROUTE_LANE0 + gsel * EXPERTS_PER_GROUP
    el = jnp.where((lane >= lo) & (lane < lo + EXPERTS_PER_GROUP), logits, NEG_INF)
    m1 = jnp.max(el, axis=-1, keepdims=True)
    i1 = jnp.min(jnp.where(el == m1, lane, LANES), axis=-1, keepdims=True)
    el2 = jnp.where(lane == i1, NEG_INF, el)
    m2 = jnp.max(el2, axis=-1, keepdims=True)
    i2 = jnp.min(jnp.where(el2 == m2, lane, LANES), axis=-1, keepdims=True)
    r = jnp.exp(m2 - m1)
    gate1 = gw / (1.0 + r)
    gate2 = gw * r / (1.0 + r)

    oh = ((lane == i1) | (lane == i2)).astype(bf16)
    row = lax.broadcasted_iota(i32, (tm, tm), 0)
    col = lax.broadcasted_iota(i32, (tm, tm), 1)
    tri = (col < row).astype(bf16)
    before = carry_ref[...] + jnp.dot(tri, oh, preferred_element_type=f32)
    r1 = jnp.sum(jnp.where(lane == i1, before, 0.0), axis=-1, keepdims=True).astype(i32)
    r2 = jnp.sum(jnp.where(lane == i2, before, 0.0), axis=-1, keepdims=True).astype(i32)
    carry = carry_ref[...] + jnp.sum(oh.astype(f32), axis=0, keepdims=True)
    carry_ref[...] = carry
    cnt_ref[...] = carry

    ri_ref[...] = jnp.where(lane == 0, i1 - ROUTE_LANE0, jnp.where(lane == 1, i2 - ROUTE_LANE0,
                            jnp.where(lane == 2, r1, jnp.where(lane == 3, r2, 0))))
    rf_ref[...] = jnp.where(lane == 0, gate1, jnp.where(lane == 1, gate2, 0.0))


def _merge(oa, on, proj, x, wa, wb, wo, g2, wrh, wrl, br, tm=256):
    n, d = x.shape
    tm = min(tm, n)
    aw = oa.shape[1]
    const = lambda shape: pl.BlockSpec(shape, lambda i: (0,) * len(shape), pipeline_mode=pl.Buffered(1))
    return pl.pallas_call(
        _merge_body,
        grid=(n // tm,),
        in_specs=[pl.BlockSpec((tm, aw), lambda i: (i, 0)),
                  pl.BlockSpec((tm, aw), lambda i: (i, 0)),
                  pl.BlockSpec((pl.Element(tm), pl.Element(d)), lambda i: (i * tm, GATE0 * HEAD_DIM)),
                  pl.BlockSpec((pl.Element(tm), pl.Element(d)), lambda i: (i * tm, GATE0 * HEAD_DIM + d)),
                  pl.BlockSpec((tm, d), lambda i: (i, 0)),
                  const((aw, d)), const((aw, d)), const((d, d)), const((1, d)),
                  const((d, LANES)), const((d, LANES)), const((1, LANES))],
        out_specs=[pl.BlockSpec((tm, d), lambda i: (i, 0)),
                   pl.BlockSpec((tm, d), lambda i: (i, 0)),
                   pl.BlockSpec((tm, LANES), lambda i: (i, 0)),
                   pl.BlockSpec((tm, LANES), lambda i: (i, 0)),
                   pl.BlockSpec((1, LANES), lambda i: (0, 0))],
        out_shape=[jax.ShapeDtypeStruct((n, d), f32), jax.ShapeDtypeStruct((n, d), f32),
                   jax.ShapeDtypeStruct((n, LANES), i32), jax.ShapeDtypeStruct((n, LANES), f32),
                   jax.ShapeDtypeStruct((1, LANES), f32)],
        scratch_shapes=[pltpu.VMEM((1, LANES), f32)],
        compiler_params=_cparams(("arbitrary",)),
        name="merge_route",
    )(oa, on, proj, proj, x, wa, wb, wo, g2, wrh, wrl, br)


def _row_copy(src_ref, src_row, dst_ref, dst_row, sem):
    return pltpu.make_async_copy(src_ref.at[pl.ds(src_row, 1), :], dst_ref.at[pl.ds(dst_row, 1), :], sem)


def _dispatch_body(d1_ref, d2_ref, h_ref, xin_ref, xbuf_ref, sem):
    del xin_ref
    tm = h_ref.shape[0]
    base = pl.program_id(0) * tm

    def issue(t, c):
        _row_copy(h_ref, t, xbuf_ref, d1_ref[base + t], sem).start()
        _row_copy(h_ref, t, xbuf_ref, d2_ref[base + t], sem).start()
        return c

    def drain(t, c):
        _row_copy(h_ref, t, xbuf_ref, d1_ref[base + t], sem).wait()
        _row_copy(h_ref, t, xbuf_ref, d2_ref[base + t], sem).wait()
        return c

    lax.fori_loop(0, tm, issue, 0)
    lax.fori_loop(0, tm, drain, 0)


def _dispatch(d1, d2, h2, n_rows, tm=256):
    n, d = h2.shape
    tm = min(tm, n)
    xzero = jnp.zeros((n_rows, d), f32)
    return pl.pallas_call(
        _dispatch_body,
        grid_spec=pltpu.PrefetchScalarGridSpec(
            num_scalar_prefetch=2,
            grid=(n // tm,),
            in_specs=[pl.BlockSpec((tm, d), lambda i, d1, d2: (i, 0)),
                      pl.BlockSpec(memory_space=pl.ANY)],
            out_specs=pl.BlockSpec(memory_space=pl.ANY),
            scratch_shapes=[pltpu.SemaphoreType.DMA(())]),
        out_shape=jax.ShapeDtypeStruct((n_rows, d), f32),
        input_output_aliases={3: 0},
        compiler_params=_cparams(("arbitrary",)),
        name="moe_dispatch",
    )(d1, d2, h2, xzero)


def _expert_body(be_ref, nu_ref, x_ref, wi_ref, wd_ref, y_ref):
    @pl.when(pl.program_id(0) < nu_ref[0])
    def _():
        h = jnp.dot(x_ref[...].astype(bf16), wi_ref[0], preferred_element_type=f32)
        a, u = h[:, :D_EXPERT], h[:, D_EXPERT:]
        act = (a * jax.nn.sigmoid(a) * u).astype(bf16)
        y_ref[...] = jnp.dot(act, wd_ref[0], preferred_element_type=f32)

    @pl.when(pl.program_id(0) >= nu_ref[0])
    def _():
        y_ref[...] = jnp.zeros_like(y_ref)


def _experts(blk_expert, n_used, xbuf, wi, wd):
    n_rows, d = xbuf.shape
    tb = EXPERT_BLOCK
    return pl.pallas_call(
        _expert_body,
        grid_spec=pltpu.PrefetchScalarGridSpec(
            num_scalar_prefetch=2,
            grid=(n_rows // tb,),
            in_specs=[pl.BlockSpec((tb, d), lambda b, be, nu: (b, 0)),
                      pl.BlockSpec((1, d, 2 * D_EXPERT), lambda b, be, nu: (be[b], 0, 0)),
                      pl.BlockSpec((1, D_EXPERT, d), lambda b, be, nu: (be[b], 0, 0))],
            out_specs=pl.BlockSpec((tb, d), lambda b, be, nu: (b, 0))),
        out_shape=jax.ShapeDtypeStruct((n_rows, d), f32),
        compiler_params=_cparams(("arbitrary",)),
        name="moe_experts",
    )(blk_expert, n_used, xbuf, wi, wd)


def _combine_body(d1_ref, d2_ref, y_ref, x1_ref, rf_ref, gf_ref, o_ref, y1_buf, y2_buf, sem):
    tm = x1_ref.shape[0]
    base = pl.program_id(0) * tm

    def issue(t, c):
        _row_copy(y_ref, d1_ref[base + t], y1_buf, t, sem).start()
        _row_copy(y_ref, d2_ref[base + t], y2_buf, t, sem).start()
        return c

    def drain(t, c):
        _row_copy(y_ref, d1_ref[base + t], y1_buf, t, sem).wait()
        _row_copy(y_ref, d2_ref[base + t], y2_buf, t, sem).wait()
        return c

    lax.fori_loop(0, tm, issue, 0)
    lax.fori_loop(0, tm, drain, 0)
    rf = rf_ref[...]
    y = y1_buf[...] * rf[:, 0:1] + y2_buf[...] * rf[:, 1:2]
    x = x1_ref[...] + y
    o_ref[...] = x * lax.rsqrt(jnp.mean(x * x, axis=-1, keepdims=True) + EPS) * gf_ref[...]


def _combine(d1, d2, ybuf, x1, rf, gf, tm=256):
    n, d = x1.shape
    tm = min(tm, n)
    return pl.pallas_call(
        _combine_body,
        grid_spec=pltpu.PrefetchScalarGridSpec(
            num_scalar_prefetch=2,
            grid=(n // tm,),
            in_specs=[pl.BlockSpec(memory_space=pl.ANY),
                      pl.BlockSpec((tm, d), lambda i, d1, d2: (i, 0)),
                      pl.BlockSpec((tm, LANES), lambda i, d1, d2: (i, 0)),
                      pl.BlockSpec((1, d), lambda i, d1, d2: (0, 0))],
            out_specs=pl.BlockSpec((tm, d), lambda i, d1, d2: (i, 0)),
            scratch_shapes=[pltpu.VMEM((tm, d), f32), pltpu.VMEM((tm, d), f32), pltpu.SemaphoreType.DMA(())]),
        out_shape=jax.ShapeDtypeStruct((n, d), f32),
        compiler_params=_cparams(("arbitrary",)),
        name="moe_combine",
    )(d1, d2, ybuf, x1, rf, gf)


def _rope_tables(seq):
    t = jnp.arange(seq, dtype=i32)
    pos = jnp.stack([t // GRID_W, t % GRID_W], axis=-1).astype(f32)
    half = HEAD_DIM // 2
    inv_freq = ROPE_THETA ** (-jnp.arange(0, half, 2, dtype=f32) / half)
    ang = pos[:, :, None] * inv_freq[None, None, :]
    ang = jnp.concatenate([ang, ang], axis=-1).reshape(seq, HEAD_DIM)
    lane = jnp.arange(HEAD_DIM, dtype=i32)
    sign = jnp.where((lane % half) < half // 2, -1.0, 1.0).astype(f32)
    return jnp.cos(ang), jnp.sin(ang) * sign


def _layer(x, p):
    batch, seq, d = x.shape
    n = batch * seq
    x2 = x.reshape(n, d)
    proj = _inproj(x2, p["norm1_g"], p["w_in"])
    cos, sin = _rope_tables(seq)
    q, k = _qkprep(proj, cos, sin, p["q_norm_g"], p["k_norm_g"], seq)
    oa = _gqa(q, k, proj, batch, seq)
    on = _na(proj, _na_bias_tables(p["na_rpb"], seq // GRID_W), batch, seq)
    x1, h2, ri, rf, cnt = _merge(oa, on, proj, x2, p["w_branch_a"], p["w_branch_b"], p["w_out"], p["norm2_g"],
                                 p["w_router_hi"], p["w_router_lo"], p["b_router"])

    tb = EXPERT_BLOCK
    n_rows = 2 * n + N_EXPERTS * tb
    counts = cnt[0, ROUTE_LANE0:ROUTE_LANE0 + N_EXPERTS].astype(i32)
    padded = (counts + tb - 1) // tb * tb
    pad_end = jnp.cumsum(padded)
    pad_start = pad_end - padded
    d1 = pad_start[ri[:, 0]] + ri[:, 2]
    d2 = pad_start[ri[:, 1]] + ri[:, 3]
    nb = n_rows // tb
    blk_expert = jnp.clip(jnp.searchsorted(pad_end, jnp.arange(nb, dtype=i32) * tb, side="right"),
                          0, N_EXPERTS - 1).astype(i32)
    n_used = (pad_end[-1:] // tb).astype(i32)

    xbuf = _dispatch(d1, d2, h2, n_rows)
    ybuf = _experts(blk_expert, n_used, xbuf, p["w_exp_in"], p["w_exp_down"])
    out = _combine(d1, d2, ybuf, x1, rf, p["norm_f_g"])
    return out.reshape(batch, seq, d)


def kernel(x_prompt, x_sample, norm1_g, w_in, q_norm_g, k_norm_g, na_rpb, w_branch_a, w_branch_b, w_out, norm2_g,
           w_router_group, b_router_group, w_router_expert, b_router_expert, w_exp_in, w_exp_down, norm_f_g):
    assert norm1_g.shape[0] == 1, "one encoder layer"
    w_r = jnp.zeros((D_MODEL, LANES), f32)
    w_r = w_r.at[:, :N_GROUPS].set(w_router_group[0]).at[:, ROUTE_LANE0:ROUTE_LANE0 + N_EXPERTS].set(w_router_expert[0])
    b_r = jnp.zeros((1, LANES), f32)
    b_r = b_r.at[0, :N_GROUPS].set(b_router_group[0]).at[0, ROUTE_LANE0:ROUTE_LANE0 + N_EXPERTS].set(b_router_expert[0])
    w_r_hi = w_r.astype(bf16)
    p = {
        "norm1_g": norm1_g[0][None], "w_in": w_in[0].astype(bf16),
        "q_norm_g": q_norm_g[0][None], "k_norm_g": k_norm_g[0][None], "na_rpb": na_rpb[0],
        "w_branch_a": w_branch_a[0].astype(bf16), "w_branch_b": w_branch_b[0].astype(bf16),
        "w_out": w_out[0].astype(bf16), "norm2_g": norm2_g[0][None],
        "w_router_hi": w_r_hi, "w_router_lo": (w_r - w_r_hi.astype(f32)).astype(bf16), "b_router": b_r,
        "w_exp_in": w_exp_in[0].astype(bf16), "w_exp_down": w_exp_down[0].astype(bf16),
        "norm_f_g": norm_f_g[None],
    }
    return _layer(x_prompt, p), _layer(x_sample, p)
```

```python
import functools

import jax
import jax.numpy as jnp
from jax import lax
from jax.experimental import pallas as pl
from jax.experimental.pallas import tpu as pltpu

f32 = jnp.float32
bf16 = jnp.bfloat16
i32 = jnp.int32

D_MODEL = 2048
GRID_W = 64
HEAD_DIM = 128
A_HEADS = 8
A_KV_HEADS = 2
A_GROUP = A_HEADS // A_KV_HEADS
NA_HEADS = 8
NA_WIN_ROWS = 8
NA_WIN_COLS = 16
Q_BLOCK = 128
ROPE_THETA = 10000.0
N_GROUPS = 4
EXPERTS_PER_GROUP = 8
N_EXPERTS = N_GROUPS * EXPERTS_PER_GROUP
D_EXPERT = D_MODEL // 4
EPS = 1e-6
NEG_INF = -1e30
SCALE = HEAD_DIM ** -0.5
LOG2E = 1.4426950408889634

QA0, KA0, VA0, QN0, KN0, VN0, GATE0 = 0, 8, 10, 12, 20, 28, 36
IN_COLS = (GATE0 + 2 * D_MODEL // HEAD_DIM) * HEAD_DIM

LANES = 128
NA_BAND_ROWS = 10
NA_KEYS = NA_BAND_ROWS * GRID_W
ROUTE_LANE0 = 4
EXPERT_BLOCK = 256
VMEM_LIMIT = 56 * 1024 * 1024


def _cparams(sem):
    return pltpu.CompilerParams(dimension_semantics=sem, vmem_limit_bytes=VMEM_LIMIT)


def _inproj_body(x_ref, g_ref, w_ref, o_ref, h_ref):
    @pl.when(pl.program_id(1) == 0)
    def _():
        x = x_ref[...]
        ms = jnp.mean(x * x, axis=-1, keepdims=True)
        h_ref[...] = (x * lax.rsqrt(ms + EPS) * g_ref[...]).astype(bf16)

    o_ref[...] = jnp.dot(h_ref[...], w_ref[...], preferred_element_type=f32).astype(o_ref.dtype)


def _inproj(x, g, w, tm=1024, tn=512):
    n, d = x.shape
    c = w.shape[1]
    tm = min(tm, n)
    return pl.pallas_call(
        _inproj_body,
        grid=(n // tm, c // tn),
        in_specs=[pl.BlockSpec((tm, d), lambda i, j: (i, 0)),
                  pl.BlockSpec((1, d), lambda i, j: (0, 0)),
                  pl.BlockSpec((d, tn), lambda i, j: (0, j))],
        out_specs=pl.BlockSpec((tm, tn), lambda i, j: (i, j)),
        out_shape=jax.ShapeDtypeStruct((n, c), bf16),
        scratch_shapes=[pltpu.VMEM((tm, d), bf16)],
        compiler_params=_cparams(("parallel", "arbitrary")),
        name="inproj",
    )(x, g, w)


def _qkprep_body(q_ref, k_ref, cos_ref, sin_ref, qg_ref, kg_ref, qo_ref, ko_ref):
    cos = cos_ref[...]
    sin = sin_ref[...]
    lane = lax.broadcasted_iota(i32, cos.shape, 1)
    first = (lane % (HEAD_DIM // 2)) < (HEAD_DIM // 4)

    def prep(x, g, scale):
        x = x.astype(f32)
        y = x * lax.rsqrt(jnp.mean(x * x, axis=-1, keepdims=True) + EPS) * g
        rot = jnp.where(first, pltpu.roll(y, HEAD_DIM - HEAD_DIM // 4, 1), pltpu.roll(y, HEAD_DIM // 4, 1))
        return (y * cos + rot * sin) * scale

    for h in range(A_HEADS):
        sl = slice(h * HEAD_DIM, (h + 1) * HEAD_DIM)
        qo_ref[:, sl] = prep(q_ref[:, sl], qg_ref[...], SCALE * LOG2E).astype(bf16)
    for h in range(A_KV_HEADS):
        sl = slice(h * HEAD_DIM, (h + 1) * HEAD_DIM)
        ko_ref[:, sl] = prep(k_ref[:, sl], kg_ref[...], 1.0).astype(bf16)


def _qkprep(proj, cos, sin, qg, kg, seq, tm=512):
    n = proj.shape[0]
    tm = min(tm, seq)
    spb = seq // tm
    qw, kw = A_HEADS * HEAD_DIM, A_KV_HEADS * HEAD_DIM
    return pl.pallas_call(
        _qkprep_body,
        grid=(n // tm,),
        in_specs=[pl.BlockSpec((tm, qw), lambda i: (i, QA0 * HEAD_DIM // qw)),
                  pl.BlockSpec((tm, kw), lambda i: (i, KA0 * HEAD_DIM // kw)),
                  pl.BlockSpec((tm, HEAD_DIM), lambda i: (i % spb, 0)),
                  pl.BlockSpec((tm, HEAD_DIM), lambda i: (i % spb, 0)),
                  pl.BlockSpec((1, HEAD_DIM), lambda i: (0, 0)),
                  pl.BlockSpec((1, HEAD_DIM), lambda i: (0, 0))],
        out_specs=[pl.BlockSpec((tm, qw), lambda i: (i, 0)),
                   pl.BlockSpec((tm, kw), lambda i: (i, 0))],
        out_shape=[jax.ShapeDtypeStruct((n, qw), bf16), jax.ShapeDtypeStruct((n, kw), bf16)],
        compiler_params=_cparams(("parallel",)),
        name="qkprep",
    )(proj, proj, cos, sin, qg, kg)


def _lane_tile(x, n):
    return jnp.concatenate([x] * n, axis=1)


def _gqa_body(q_ref, k_ref, v_ref, o_ref, m_ref, acc_ref, *, tk):
    nkv = k_ref.shape[0] // tk
    m_ref[...] = jnp.full(m_ref.shape, NEG_INF, f32)
    acc_ref[...] = jnp.zeros(acc_ref.shape, f32)
    ones = jnp.ones((tk, HEAD_DIM), bf16)

    def step(j, c):
        off = pl.multiple_of(j * tk, tk)
        k = k_ref[pl.ds(off, tk), :]
        v1 = jnp.concatenate([v_ref[pl.ds(off, tk), :], ones], axis=1)

        def scores(g):
            q = q_ref[:, g * HEAD_DIM:(g + 1) * HEAD_DIM]
            return lax.dot_general(q, k, (((1,), (1,)), ((), ())), preferred_element_type=f32)

        s = scores(0)
        for g in range(A_GROUP):
            s_next = scores(g + 1) if g + 1 < A_GROUP else None
            m = m_ref[g]
            m_new = jnp.maximum(m, jnp.max(s, axis=-1, keepdims=True))
            alpha = jnp.exp2(m - m_new)
            p = jnp.exp2(s - _lane_tile(m_new, tk // LANES))
            pv = jnp.dot(p.astype(bf16), v1, preferred_element_type=f32)
            acc_ref[g] = _lane_tile(alpha, 2) * acc_ref[g] + pv
            m_ref[g] = m_new
            s = s_next
        return c

    lax.fori_loop(0, nkv, step, 0)
    for g in range(A_GROUP):
        acc = acc_ref[g]
        o_ref[:, g * HEAD_DIM:(g + 1) * HEAD_DIM] = (acc[:, :HEAD_DIM] / acc[:, HEAD_DIM:]).astype(o_ref.dtype)


def _gqa(q, k, proj, batch, seq, tq=512, tk=1024):
    n = q.shape[0]
    tq, tk = min(tq, seq), min(tk, seq)
    nq = seq // tq
    gw = A_GROUP * HEAD_DIM
    return pl.pallas_call(
        functools.partial(_gqa_body, tk=tk),
        grid=(batch, A_KV_HEADS, nq),
        in_specs=[pl.BlockSpec((tq, gw), lambda b, h, i: (b * nq + i, h)),
                  pl.BlockSpec((seq, HEAD_DIM), lambda b, h, i: (b, h)),
                  pl.BlockSpec((seq, HEAD_DIM), lambda b, h, i: (b, VA0 + h))],
        out_specs=pl.BlockSpec((tq, gw), lambda b, h, i: (b * nq + i, h)),
        out_shape=jax.ShapeDtypeStruct((n, A_HEADS * HEAD_DIM), bf16),
        scratch_shapes=[pltpu.VMEM((A_GROUP, tq, LANES), f32), pltpu.VMEM((A_GROUP, tq, 2 * HEAD_DIM), f32)],
        compiler_params=_cparams(("parallel", "parallel", "arbitrary")),
        name="gqa",
    )(q, k, proj)


def _na_bias_tables(rpb, rows):
    nblk = rows * GRID_W // Q_BLOCK
    rpq = Q_BLOCK // GRID_W
    ncls = 5
    blk = jnp.array([0, 1, 2, nblk - 2, nblk - 1], i32)
    col = jnp.arange(GRID_W, dtype=i32)
    col_start = jnp.clip(col - NA_WIN_COLS // 2, 0, GRID_W - NA_WIN_COLS)
    col_in = (col[None, :] >= col_start[:, None]) & (col[None, :] < col_start[:, None] + NA_WIN_COLS)
    dc = jnp.clip(col[None, :] - col[:, None] + NA_WIN_COLS - 1, 0, 2 * NA_WIN_COLS - 2)
    r0 = blk * rpq
    bs = jnp.clip(r0 - NA_WIN_ROWS // 2, 0, rows - NA_BAND_ROWS)
    q_row = r0[:, None] + jnp.arange(rpq, dtype=i32)[None, :]
    k_row = bs[:, None] + jnp.arange(NA_BAND_ROWS, dtype=i32)[None, :]
    row_start = jnp.clip(q_row - NA_WIN_ROWS // 2, 0, rows - NA_WIN_ROWS)
    row_in = (k_row[:, None, :] >= row_start[:, :, None]) & (k_row[:, None, :] < row_start[:, :, None] + NA_WIN_ROWS)
    dr = jnp.clip(k_row[:, None, :] - q_row[:, :, None] + NA_WIN_ROWS - 1, 0, 2 * NA_WIN_ROWS - 2)
    picked = rpb.astype(f32)[:, dr, :]
    onehot = (dc[None] == jnp.arange(2 * NA_WIN_COLS - 1, dtype=i32)[:, None, None]).astype(f32)
    bias = jnp.einsum("hcqkd,dxy->chqxky", picked, onehot, precision=lax.Precision.HIGHEST)
    mask = row_in[:, None, :, None, :, None] & col_in[None, None, None, :, None, :]
    bias = jnp.where(mask, bias, NEG_INF)
    return bias.reshape(ncls, NA_HEADS, Q_BLOCK, NA_KEYS)


def _na_body(q_ref, k_ref, v_ref, tab_ref, o_ref):
    for h in range(q_ref.shape[1] // HEAD_DIM):
        sl = slice(h * HEAD_DIM, (h + 1) * HEAD_DIM)
        s = lax.dot_general(q_ref[:, sl], k_ref[:, sl], (((1,), (1,)), ((), ())), preferred_element_type=f32)
        s = s * SCALE + tab_ref[0, h]
        m = jnp.max(s, axis=-1, keepdims=True)
        p = jnp.exp(s - m)
        l = jnp.sum(p, axis=-1, keepdims=True)
        o = jnp.dot(p.astype(bf16), v_ref[:, sl], preferred_element_type=f32)
        o_ref[:, sl] = (o / l).astype(o_ref.dtype)


def _na(proj, tab, batch, seq, hg=4):
    n = proj.shape[0]
    nblk = seq // Q_BLOCK
    rows = seq // GRID_W
    rpq = Q_BLOCK // GRID_W
    w = hg * HEAD_DIM
    ngrp = NA_HEADS // hg

    def band(b, i):
        first_row = jnp.clip(i * rpq - NA_WIN_ROWS // 2, 0, rows - NA_BAND_ROWS)
        return pl.multiple_of(b * seq + first_row * GRID_W, GRID_W)

    def cls(i):
        return jnp.where(i < 2, i, jnp.where(i <= nblk - 3, 2, i - (nblk - 5)))

    return pl.pallas_call(
        _na_body,
        grid=(batch, nblk, ngrp),
        in_specs=[pl.BlockSpec((Q_BLOCK, w), lambda b, i, g: (b * nblk + i, QN0 // hg + g)),
                  pl.BlockSpec((pl.Element(NA_KEYS), pl.Element(w)),
                               lambda b, i, g: (band(b, i), (KN0 // hg + g) * w)),
                  pl.BlockSpec((pl.Element(NA_KEYS), pl.Element(w)),
                               lambda b, i, g: (band(b, i), (VN0 // hg + g) * w)),
                  pl.BlockSpec((1, hg, Q_BLOCK, NA_KEYS), lambda b, i, g: (cls(i), g, 0, 0))],
        out_specs=pl.BlockSpec((Q_BLOCK, w), lambda b, i, g: (b * nblk + i, g)),
        out_shape=jax.ShapeDtypeStruct((n, NA_HEADS * HEAD_DIM), bf16),
        compiler_params=_cparams(("parallel", "parallel", "arbitrary")),
        name="natten",
    )(proj, proj, proj, tab)


def _merge_body(oa_ref, on_ref, ga_ref, gb_ref, x_ref, wa_ref, wb_ref, wo_ref, g2_ref, wrh_ref, wrl_ref, br_ref,
                x1_ref, h2_ref, ri_ref, rf_ref, cnt_ref, carry_ref):
    tm = x_ref.shape[0]

    @pl.when(pl.program_id(0) == 0)
    def _():
        carry_ref[...] = jnp.zeros_like(carry_ref)

    a = jnp.dot(oa_ref[...], wa_ref[...], preferred_element_type=f32)
    b = jnp.dot(on_ref[...], wb_ref[...], preferred_element_type=f32)
    ga = jax.nn.sigmoid(ga_ref[...].astype(f32))
    gb = jax.nn.sigmoid(gb_ref[...].astype(f32))
    merged = (ga * a + gb * b).astype(bf16)
    x1 = x_ref[...] + jnp.dot(merged, wo_ref[...], preferred_element_type=f32)
    x1_ref[...] = x1

    h2 = x1 * lax.rsqrt(jnp.mean(x1 * x1, axis=-1, keepdims=True) + EPS) * g2_ref[...]
    h2_ref[...] = h2
    h_hi = h2.astype(bf16)
    h_lo = (h2 - h_hi.astype(f32)).astype(bf16)
    logits = (jnp.dot(h_hi, wrh_ref[...], preferred_element_type=f32)
              + jnp.dot(h_lo, wrh_ref[...], preferred_element_type=f32)
              + jnp.dot(h_hi, wrl_ref[...], preferred_element_type=f32)) + br_ref[...]

    lane = lax.broadcasted_iota(i32, logits.shape, 1)
    gl = jnp.where(lane < N_GROUPS, logits, NEG_INF)
    gmax = jnp.max(gl, axis=-1, keepdims=True)
    gsel = jnp.min(jnp.where(gl == gmax, lane, LANES), axis=-1, keepdims=True)
    gw = 1.0 / jnp.sum(jnp.exp(gl - gmax), axis=-1, keepdims=True)
    lo = ROUTE_LANE0 + gsel * EXPERTS_PER_GROUP
    el = jnp.where((lane >= lo) & (lane < lo + EXPERTS_PER_GROUP), logits, NEG_INF)
    m1 = jnp.max(el, axis=-1, keepdims=True)
    i1 = jnp.min(jnp.where(el == m1, lane, LANES), axis=-1, keepdims=True)
    el2 = jnp.where(lane == i1, NEG_INF, el)
    m2 = jnp.max(el2, axis=-1, keepdims=True)
    i2 = jnp.min(jnp.where(el2 == m2, lane, LANES), axis=-1, keepdims=True)
    r = jnp.exp(m2 - m1)
    gate1 = gw / (1.0 + r)
    gate2 = gw * r / (1.0 + r)

    oh = ((lane == i1) | (lane == i2)).astype(bf16)
    row = lax.broadcasted_iota(i32, (tm, tm), 0)
    col = lax.broadcasted_iota(i32, (tm, tm), 1)
    tri = (col < row).astype(bf16)
    before = carry_ref[...] + jnp.dot(tri, oh, preferred_element_type=f32)
    r1 = jnp.sum(jnp.where(lane == i1, before, 0.0), axis=-1, keepdims=True).astype(i32)
    r2 = jnp.sum(jnp.where(lane == i2, before, 0.0), axis=-1, keepdims=True).astype(i32)
    carry = carry_ref[...] + jnp.sum(oh.astype(f32), axis=0, keepdims=True)
    carry_ref[...] = carry
    cnt_ref[...] = carry

    ri_ref[...] = jnp.where(lane == 0, i1 - ROUTE_LANE0, jnp.where(lane == 1, i2 - ROUTE_LANE0,
                            jnp.where(lane == 2, r1, jnp.where(lane == 3, r2, 0))))
    rf_ref[...] = jnp.where(lane == 0, gate1, jnp.where(lane == 1, gate2, 0.0))


def _merge(oa, on, proj, x, wa, wb, wo, g2, wrh, wrl, br, tm=256):
    n, d = x.shape
    tm = min(tm, n)
    aw = oa.shape[1]
    const = lambda shape: pl.BlockSpec(shape, lambda i: (0,) * len(shape), pipeline_mode=pl.Buffered(1))
    return pl.pallas_call(
        _merge_body,
        grid=(n // tm,),
        in_specs=[pl.BlockSpec((tm, aw), lambda i: (i, 0)),
                  pl.BlockSpec((tm, aw), lambda i: (i, 0)),
                  pl.BlockSpec((pl.Element(tm), pl.Element(d)), lambda i: (i * tm, GATE0 * HEAD_DIM)),
                  pl.BlockSpec((pl.Element(tm), pl.Element(d)), lambda i: (i * tm, GATE0 * HEAD_DIM + d)),
                  pl.BlockSpec((tm, d), lambda i: (i, 0)),
                  const((aw, d)), const((aw, d)), const((d, d)), const((1, d)),
                  const((d, LANES)), const((d, LANES)), const((1, LANES))],
        out_specs=[pl.BlockSpec((tm, d), lambda i: (i, 0)),
                   pl.BlockSpec((tm, d), lambda i: (i, 0)),
                   pl.BlockSpec((tm, LANES), lambda i: (i, 0)),
                   pl.BlockSpec((tm, LANES), lambda i: (i, 0)),
                   pl.BlockSpec((1, LANES), lambda i: (0, 0))],
        out_shape=[jax.ShapeDtypeStruct((n, d), f32), jax.ShapeDtypeStruct((n, d), f32),
                   jax.ShapeDtypeStruct((n, LANES), i32), jax.ShapeDtypeStruct((n, LANES), f32),
                   jax.ShapeDtypeStruct((1, LANES), f32)],
        scratch_shapes=[pltpu.VMEM((1, LANES), f32)],
        compiler_params=_cparams(("arbitrary",)),
        name="merge_route",
    )(oa, on, proj, proj, x, wa, wb, wo, g2, wrh, wrl, br)


def _row_copy(src_ref, src_row, dst_ref, dst_row, sem):
    return pltpu.make_async_copy(src_ref.at[pl.ds(src_row, 1), :], dst_ref.at[pl.ds(dst_row, 1), :], sem)


def _dispatch_body(d1_ref, d2_ref, h_ref, xin_ref, xbuf_ref, sem):
    del xin_ref
    tm = h_ref.shape[0]
    base = pl.program_id(0) * tm

    def issue(t, c):
        _row_copy(h_ref, t, xbuf_ref, d1_ref[base + t], sem).start()
        _row_copy(h_ref, t, xbuf_ref, d2_ref[base + t], sem).start()
        return c

    def drain(t, c):
        _row_copy(h_ref, t, xbuf_ref, d1_ref[base + t], sem).wait()
        _row_copy(h_ref, t, xbuf_ref, d2_ref[base + t], sem).wait()
        return c

    lax.fori_loop(0, tm, issue, 0)
    lax.fori_loop(0, tm, drain, 0)


def _dispatch(d1, d2, h2, n_rows, tm=256):
    n, d = h2.shape
    tm = min(tm, n)
    xzero = jnp.zeros((n_rows, d), f32)
    return pl.pallas_call(
        _dispatch_body,
        grid_spec=pltpu.PrefetchScalarGridSpec(
            num_scalar_prefetch=2,
            grid=(n // tm,),
            in_specs=[pl.BlockSpec((tm, d), lambda i, d1, d2: (i, 0)),
                      pl.BlockSpec(memory_space=pl.ANY)],
            out_specs=pl.BlockSpec(memory_space=pl.ANY),
            scratch_shapes=[pltpu.SemaphoreType.DMA(())]),
        out_shape=jax.ShapeDtypeStruct((n_rows, d), f32),
        input_output_aliases={3: 0},
        compiler_params=_cparams(("arbitrary",)),
        name="moe_dispatch",
    )(d1, d2, h2, xzero)


def _expert_body(be_ref, nu_ref, x_ref, wi_ref, wd_ref, y_ref):
    @pl.when(pl.program_id(0) < nu_ref[0])
    def _():
        h = jnp.dot(x_ref[...].astype(bf16), wi_ref[0], preferred_element_type=f32)
        a, u = h[:, :D_EXPERT], h[:, D_EXPERT:]
        act = (a * jax.nn.sigmoid(a) * u).astype(bf16)
        y_ref[...] = jnp.dot(act, wd_ref[0], preferred_element_type=f32)

    @pl.when(pl.program_id(0) >= nu_ref[0])
    def _():
        y_ref[...] = jnp.zeros_like(y_ref)


def _experts(blk_expert, n_used, xbuf, wi, wd):
    n_rows, d = xbuf.shape
    tb = EXPERT_BLOCK
    return pl.pallas_call(
        _expert_body,
        grid_spec=pltpu.PrefetchScalarGridSpec(
            num_scalar_prefetch=2,
            grid=(n_rows // tb,),
            in_specs=[pl.BlockSpec((tb, d), lambda b, be, nu: (b, 0)),
                      pl.BlockSpec((1, d, 2 * D_EXPERT), lambda b, be, nu: (be[b], 0, 0)),
                      pl.BlockSpec((1, D_EXPERT, d), lambda b, be, nu: (be[b], 0, 0))],
            out_specs=pl.BlockSpec((tb, d), lambda b, be, nu: (b, 0))),
        out_shape=jax.ShapeDtypeStruct((n_rows, d), f32),
        compiler_params=_cparams(("arbitrary",)),
        name="moe_experts",
    )(blk_expert, n_used, xbuf, wi, wd)


def _combine_body(d1_ref, d2_ref, y_ref, x1_ref, rf_ref, gf_ref, o_ref, y1_buf, y2_buf, sem):
    tm = x1_ref.shape[0]
    base = pl.program_id(0) * tm

    def issue(t, c):
        _row_copy(y_ref, d1_ref[base + t], y1_buf, t, sem).start()
        _row_copy(y_ref, d2_ref[base + t], y2_buf, t, sem).start()
        return c

    def drain(t, c):
        _row_copy(y_ref, d1_ref[base + t], y1_buf, t, sem).wait()
        _row_copy(y_ref, d2_ref[base + t], y2_buf, t, sem).wait()
        return c

    lax.fori_loop(0, tm, issue, 0)
    lax.fori_loop(0, tm, drain, 0)
    rf = rf_ref[...]
    y = y1_buf[...] * rf[:, 0:1] + y2_buf[...] * rf[:, 1:2]
    x = x1_ref[...] + y
    o_ref[...] = x * lax.rsqrt(jnp.mean(x * x, axis=-1, keepdims=True) + EPS) * gf_ref[...]


def _combine(d1, d2, ybuf, x1, rf, gf, tm=256):
    n, d = x1.shape
    tm = min(tm, n)
    return pl.pallas_call(
        _combine_body,
        grid_spec=pltpu.PrefetchScalarGridSpec(
            num_scalar_prefetch=2,
            grid=(n // tm,),
            in_specs=[pl.BlockSpec(memory_space=pl.ANY),
                      pl.BlockSpec((tm, d), lambda i, d1, d2: (i, 0)),
                      pl.BlockSpec((tm, LANES), lambda i, d1, d2: (i, 0)),
                      pl.BlockSpec((1, d), lambda i, d1, d2: (0, 0))],
            out_specs=pl.BlockSpec((tm, d), lambda i, d1, d2: (i, 0)),
            scratch_shapes=[pltpu.VMEM((tm, d), f32), pltpu.VMEM((tm, d), f32), pltpu.SemaphoreType.DMA(())]),
        out_shape=jax.ShapeDtypeStruct((n, d), f32),
        compiler_params=_cparams(("arbitrary",)),
        name="moe_combine",
    )(d1, d2, ybuf, x1, rf, gf)


def _rope_tables(seq):
    t = jnp.arange(seq, dtype=i32)
    pos = jnp.stack([t // GRID_W, t % GRID_W], axis=-1).astype(f32)
    half = HEAD_DIM // 2
    inv_freq = ROPE_THETA ** (-jnp.arange(0, half, 2, dtype=f32) / half)
    ang = pos[:, :, None] * inv_freq[None, None, :]
    ang = jnp.concatenate([ang, ang], axis=-1).reshape(seq, HEAD_DIM)
    lane = jnp.arange(HEAD_DIM, dtype=i32)
    sign = jnp.where((lane % half) < half // 2, -1.0, 1.0).astype(f32)
    return jnp.cos(ang), jnp.sin(ang) * sign


def _layer(x, p):
    batch, seq, d = x.shape
    n = batch * seq
    x2 = x.reshape(n, d)
    proj = _inproj(x2, p["norm1_g"], p["w_in"])
    cos, sin = _rope_tables(seq)
    q, k = _qkprep(proj, cos, sin, p["q_norm_g"], p["k_norm_g"], seq)
    oa = _gqa(q, k, proj, batch, seq)
    on = _na(proj, _na_bias_tables(p["na_rpb"], seq // GRID_W), batch, seq)
    x1, h2, ri, rf, cnt = _merge(oa, on, proj, x2, p["w_branch_a"], p["w_branch_b"], p["w_out"], p["norm2_g"],
                                 p["w_router_hi"], p["w_router_lo"], p["b_router"])

    tb = EXPERT_BLOCK
    n_rows = 2 * n + N_EXPERTS * tb
    counts = cnt[0, ROUTE_LANE0:ROUTE_LANE0 + N_EXPERTS].astype(i32)
    padded = (counts + tb - 1) // tb * tb
    pad_end = jnp.cumsum(padded)
    pad_start = pad_end - padded
    d1 = pad_start[ri[:, 0]] + ri[:, 2]
    d2 = pad_start[ri[:, 1]] + ri[:, 3]
    nb = n_rows // tb
    blk_expert = jnp.clip(jnp.searchsorted(pad_end, jnp.arange(nb, dtype=i32) * tb, side="right"),
                          0, N_EXPERTS - 1).astype(i32)
    n_used = (pad_end[-1:] // tb).astype(i32)

    xbuf = _dispatch(d1, d2, h2, n_rows)
    ybuf = _experts(blk_expert, n_used, xbuf, p["w_exp_in"], p["w_exp_down"])
    out = _combine(d1, d2, ybuf, x1, rf, p["norm_f_g"])
    return out.reshape(batch, seq, d)


def kernel(x_prompt, x_sample, norm1_g, w_in, q_norm_g, k_norm_g, na_rpb, w_branch_a, w_branch_b, w_out, norm2_g,
           w_router_group, b_router_group, w_router_expert, b_router_expert, w_exp_in, w_exp_down, norm_f_g):
    assert norm1_g.shape[0] == 1, "one encoder layer"
    w_r = jnp.zeros((D_MODEL, LANES), f32)
    w_r = w_r.at[:, :N_GROUPS].set(w_router_group[0]).at[:, ROUTE_LANE0:ROUTE_LANE0 + N_EXPERTS].set(w_router_expert[0])
    b_r = jnp.zeros((1, LANES), f32)
    b_r = b_r.at[0, :N_GROUPS].set(b_router_group[0]).at[0, ROUTE_LANE0:ROUTE_LANE0 + N_EXPERTS].set(b_router_expert[0])
    w_r_hi = w_r.astype(bf16)
    p = {
        "norm1_g": norm1_g[0][None], "w_in": w_in[0].astype(bf16),
        "q_norm_g": q_norm_g[0][None], "k_norm_g": k_norm_g[0][None], "na_rpb": na_rpb[0],
        "w_branch_a": w_branch_a[0].astype(bf16), "w_branch_b": w_branch_b[0].astype(bf16),
        "w_out": w_out[0].astype(bf16), "norm2_g": norm2_g[0][None],
        "w_router_hi": w_r_hi, "w_router_lo": (w_r - w_r_hi.astype(f32)).astype(bf16), "b_router": b_r,
        "w_exp_in": w_exp_in[0].astype(bf16), "w_exp_down": w_exp_down[0].astype(bf16),
        "norm_f_g": norm_f_g[None],
    }
    return _layer(x_prompt, p), _layer(x_sample, p)
```

```python
import functools

import jax
import jax.numpy as jnp
from jax import lax
from jax.experimental import pallas as pl
from jax.experimental.pallas import tpu as pltpu

f32 = jnp.float32
bf16 = jnp.bfloat16
i32 = jnp.int32

D_MODEL = 2048
GRID_W = 64
HEAD_DIM = 128
A_HEADS = 8
A_KV_HEADS = 2
A_GROUP = A_HEADS // A_KV_HEADS
NA_HEADS = 8
NA_WIN_ROWS = 8
NA_WIN_COLS = 16
Q_BLOCK = 128
ROPE_THETA = 10000.0
N_GROUPS = 4
EXPERTS_PER_GROUP = 8
N_EXPERTS = N_GROUPS * EXPERTS_PER_GROUP
D_EXPERT = D_MODEL // 4
EPS = 1e-6
NEG_INF = -1e30
SCALE = HEAD_DIM ** -0.5
LOG2E = 1.4426950408889634

QA0, KA0, VA0, QN0, KN0, VN0, GATE0 = 0, 8, 10, 12, 20, 28, 36
IN_COLS = (GATE0 + 2 * D_MODEL // HEAD_DIM) * HEAD_DIM

LANES = 128
NA_BAND_ROWS = 10
NA_KEYS = NA_BAND_ROWS * GRID_W
ROUTE_LANE0 = 4
EXPERT_BLOCK = 256
DMA_UNROLL = 8
VMEM_LIMIT = 56 * 1024 * 1024


def _cparams(sem):
    return pltpu.CompilerParams(dimension_semantics=sem, vmem_limit_bytes=VMEM_LIMIT)


def _inproj_body(x_ref, g_ref, w_ref, o_ref, h_ref):
    @pl.when(pl.program_id(1) == 0)
    def _():
        x = x_ref[...]
        ms = jnp.mean(x * x, axis=-1, keepdims=True)
        h_ref[...] = (x * lax.rsqrt(ms + EPS) * g_ref[...]).astype(bf16)

    o_ref[...] = jnp.dot(h_ref[...], w_ref[...], preferred_element_type=f32).astype(o_ref.dtype)


def _inproj(x, g, w, tm=1024, tn=IN_COLS // 4):
    n, d = x.shape
    c = w.shape[1]
    tm = min(tm, n)
    return pl.pallas_call(
        _inproj_body,
        grid=(n // tm, c // tn),
        in_specs=[pl.BlockSpec((tm, d), lambda i, j: (i, 0)),
                  pl.BlockSpec((1, d), lambda i, j: (0, 0)),
                  pl.BlockSpec((d, tn), lambda i, j: (0, j))],
        out_specs=pl.BlockSpec((tm, tn), lambda i, j: (i, j)),
        out_shape=jax.ShapeDtypeStruct((n, c), bf16),
        scratch_shapes=[pltpu.VMEM((tm, d), bf16)],
        compiler_params=_cparams(("parallel", "arbitrary")),
        name="inproj",
    )(x, g, w)


def _qkprep_body(q_ref, k_ref, cos_ref, sin_ref, qg_ref, kg_ref, qo_ref, ko_ref):
    cos = cos_ref[...]
    sin = sin_ref[...]
    lane = lax.broadcasted_iota(i32, cos.shape, 1)
    first = (lane % (HEAD_DIM // 2)) < (HEAD_DIM // 4)

    def prep(x, g, scale):
        x = x.astype(f32)
        y = x * lax.rsqrt(jnp.mean(x * x, axis=-1, keepdims=True) + EPS) * g
        rot = jnp.where(first, pltpu.roll(y, HEAD_DIM - HEAD_DIM // 4, 1), pltpu.roll(y, HEAD_DIM // 4, 1))
        return (y * cos + rot * sin) * scale

    for h in range(A_HEADS):
        sl = slice(h * HEAD_DIM, (h + 1) * HEAD_DIM)
        qo_ref[:, sl] = prep(q_ref[:, sl], qg_ref[...], SCALE * LOG2E).astype(bf16)
    for h in range(A_KV_HEADS):
        sl = slice(h * HEAD_DIM, (h + 1) * HEAD_DIM)
        ko_ref[:, sl] = prep(k_ref[:, sl], kg_ref[...], 1.0).astype(bf16)


def _qkprep(proj, cos, sin, qg, kg, seq, tm=512):
    n = proj.shape[0]
    tm = min(tm, seq)
    spb = seq // tm
    qw, kw = A_HEADS * HEAD_DIM, A_KV_HEADS * HEAD_DIM
    return pl.pallas_call(
        _qkprep_body,
        grid=(n // tm,),
        in_specs=[pl.BlockSpec((tm, qw), lambda i: (i, QA0 * HEAD_DIM // qw)),
                  pl.BlockSpec((tm, kw), lambda i: (i, KA0 * HEAD_DIM // kw)),
                  pl.BlockSpec((tm, HEAD_DIM), lambda i: (i % spb, 0)),
                  pl.BlockSpec((tm, HEAD_DIM), lambda i: (i % spb, 0)),
                  pl.BlockSpec((1, HEAD_DIM), lambda i: (0, 0)),
                  pl.BlockSpec((1, HEAD_DIM), lambda i: (0, 0))],
        out_specs=[pl.BlockSpec((tm, qw), lambda i: (i, 0)),
                   pl.BlockSpec((tm, kw), lambda i: (i, 0))],
        out_shape=[jax.ShapeDtypeStruct((n, qw), bf16), jax.ShapeDtypeStruct((n, kw), bf16)],
        compiler_params=_cparams(("parallel",)),
        name="qkprep",
    )(proj, proj, cos, sin, qg, kg)


def _lane_tile(x, n):
    return jnp.concatenate([x] * n, axis=1)


def _gqa_body(q_ref, k_ref, v_ref, o_ref, m_ref, acc_ref, *, tk):
    nkv = k_ref.shape[0] // tk
    m_ref[...] = jnp.full(m_ref.shape, NEG_INF, f32)
    acc_ref[...] = jnp.zeros(acc_ref.shape, f32)
    ones = jnp.ones((tk, HEAD_DIM), bf16)

    def step(j, c):
        off = pl.multiple_of(j * tk, tk)
        k = k_ref[pl.ds(off, tk), :]
        v1 = jnp.concatenate([v_ref[pl.ds(off, tk), :], ones], axis=1)

        def scores(g):
            q = q_ref[:, g * HEAD_DIM:(g + 1) * HEAD_DIM]
            return lax.dot_general(q, k, (((1,), (1,)), ((), ())), preferred_element_type=f32)

        s = scores(0)
        for g in range(A_GROUP):
            s_next = scores(g + 1) if g + 1 < A_GROUP else None
            m = m_ref[g]
            m_new = jnp.maximum(m, jnp.max(s, axis=-1, keepdims=True))
            alpha = jnp.exp2(m - m_new)
            p = jnp.exp2(s - _lane_tile(m_new, tk // LANES))
            pv = jnp.dot(p.astype(bf16), v1, preferred_element_type=f32)
            acc_ref[g] = _lane_tile(alpha, 2) * acc_ref[g] + pv
            m_ref[g] = m_new
            s = s_next
        return c

    lax.fori_loop(0, nkv, step, 0)
    for g in range(A_GROUP):
        acc = acc_ref[g]
        o_ref[:, g * HEAD_DIM:(g + 1) * HEAD_DIM] = (acc[:, :HEAD_DIM] / acc[:, HEAD_DIM:]).astype(o_ref.dtype)


def _gqa(q, k, proj, batch, seq, tq=512, tk=1024):
    n = q.shape[0]
    tq, tk = min(tq, seq), min(tk, seq)
    nq = seq // tq
    gw = A_GROUP * HEAD_DIM
    return pl.pallas_call(
        functools.partial(_gqa_body, tk=tk),
        grid=(batch, A_KV_HEADS, nq),
        in_specs=[pl.BlockSpec((tq, gw), lambda b, h, i: (b * nq + i, h)),
                  pl.BlockSpec((seq, HEAD_DIM), lambda b, h, i: (b, h)),
                  pl.BlockSpec((seq, HEAD_DIM), lambda b, h, i: (b, VA0 + h))],
        out_specs=pl.BlockSpec((tq, gw), lambda b, h, i: (b * nq + i, h)),
        out_shape=jax.ShapeDtypeStruct((n, A_HEADS * HEAD_DIM), bf16),
        scratch_shapes=[pltpu.VMEM((A_GROUP, tq, LANES), f32), pltpu.VMEM((A_GROUP, tq, 2 * HEAD_DIM), f32)],
        compiler_params=_cparams(("parallel", "parallel", "arbitrary")),
        name="gqa",
    )(q, k, proj)


def _na_bias_tables(rpb, rows):
    nblk = rows * GRID_W // Q_BLOCK
    rpq = Q_BLOCK // GRID_W
    ncls = 5
    blk = jnp.array([0, 1, 2, nblk - 2, nblk - 1], i32)
    col = jnp.arange(GRID_W, dtype=i32)
    col_start = jnp.clip(col - NA_WIN_COLS // 2, 0, GRID_W - NA_WIN_COLS)
    col_in = (col[None, :] >= col_start[:, None]) & (col[None, :] < col_start[:, None] + NA_WIN_COLS)
    dc = jnp.clip(col[None, :] - col[:, None] + NA_WIN_COLS - 1, 0, 2 * NA_WIN_COLS - 2)
    r0 = blk * rpq
    bs = jnp.clip(r0 - NA_WIN_ROWS // 2, 0, rows - NA_BAND_ROWS)
    q_row = r0[:, None] + jnp.arange(rpq, dtype=i32)[None, :]
    k_row = bs[:, None] + jnp.arange(NA_BAND_ROWS, dtype=i32)[None, :]
    row_start = jnp.clip(q_row - NA_WIN_ROWS // 2, 0, rows - NA_WIN_ROWS)
    row_in = (k_row[:, None, :] >= row_start[:, :, None]) & (k_row[:, None, :] < row_start[:, :, None] + NA_WIN_ROWS)
    dr = jnp.clip(k_row[:, None, :] - q_row[:, :, None] + NA_WIN_ROWS - 1, 0, 2 * NA_WIN_ROWS - 2)
    picked = rpb.astype(f32)[:, dr, :]
    onehot = (dc[None] == jnp.arange(2 * NA_WIN_COLS - 1, dtype=i32)[:, None, None]).astype(f32)
    bias = jnp.einsum("hcqkd,dxy->chqxky", picked, onehot, precision=lax.Precision.HIGHEST)
    mask = row_in[:, None, :, None, :, None] & col_in[None, None, None, :, None, :]
    bias = jnp.where(mask, bias * LOG2E, NEG_INF)
    return bias.reshape(ncls, NA_HEADS, Q_BLOCK, NA_KEYS)


def _na_body(q_ref, k_ref, v_ref, tab_ref, o_ref):
    nh = q_ref.shape[1] // HEAD_DIM
    ones = jnp.ones((NA_KEYS, HEAD_DIM), bf16)

    def scores(h):
        sl = slice(h * HEAD_DIM, (h + 1) * HEAD_DIM)
        s = lax.dot_general(q_ref[:, sl], k_ref[:, sl], (((1,), (1,)), ((), ())), preferred_element_type=f32)
        return s * (SCALE * LOG2E) + tab_ref[0, h]

    s = scores(0)
    for h in range(nh):
        sl = slice(h * HEAD_DIM, (h + 1) * HEAD_DIM)
        s_next = scores(h + 1) if h + 1 < nh else None
        m = jnp.broadcast_to(jnp.max(s, axis=-1, keepdims=True), (Q_BLOCK, LANES))
        p = jnp.exp2(s - _lane_tile(m, NA_KEYS // LANES))
        v1 = jnp.concatenate([v_ref[:, sl], ones], axis=1)
        pv = jnp.dot(p.astype(bf16), v1, preferred_element_type=f32)
        o_ref[:, sl] = (pv[:, :HEAD_DIM] / pv[:, HEAD_DIM:]).astype(o_ref.dtype)
        s = s_next


def _na(proj, tab, batch, seq):
    n = proj.shape[0]
    nblk = seq // Q_BLOCK
    rows = seq // GRID_W
    rpq = Q_BLOCK // GRID_W
    w = NA_HEADS * HEAD_DIM

    def band(b, i):
        first_row = jnp.clip(i * rpq - NA_WIN_ROWS // 2, 0, rows - NA_BAND_ROWS)
        return pl.multiple_of(b * seq + first_row * GRID_W, GRID_W)

    def cls(i):
        return jnp.where(i < 2, i, jnp.where(i <= nblk - 3, 2, i - (nblk - 5)))

    return pl.pallas_call(
        _na_body,
        grid=(batch, nblk),
        in_specs=[pl.BlockSpec((pl.Element(Q_BLOCK), pl.Element(w)),
                               lambda b, i: ((b * nblk + i) * Q_BLOCK, QN0 * HEAD_DIM)),
                  pl.BlockSpec((pl.Element(NA_KEYS), pl.Element(w)), lambda b, i: (band(b, i), KN0 * HEAD_DIM)),
                  pl.BlockSpec((pl.Element(NA_KEYS), pl.Element(w)), lambda b, i: (band(b, i), VN0 * HEAD_DIM)),
                  pl.BlockSpec((1, NA_HEADS, Q_BLOCK, NA_KEYS), lambda b, i: (cls(i), 0, 0, 0))],
        out_specs=pl.BlockSpec((Q_BLOCK, w), lambda b, i: (b * nblk + i, 0)),
        out_shape=jax.ShapeDtypeStruct((n, w), bf16),
        compiler_params=_cparams(("parallel", "arbitrary")),
        name="natten",
    )(proj, proj, proj, tab)


def _merge_body(oa_ref, on_ref, ga_ref, gb_ref, x_ref, wa_ref, wb_ref, wo_ref, g2_ref, wr_ref, br_ref,
                x1_ref, h2_ref, ri_ref, rf_ref, cnt_ref, carry_ref):
    tm = x_ref.shape[0]

    @pl.when(pl.program_id(0) == 0)
    def _():
        carry_ref[...] = jnp.zeros_like(carry_ref)

    a = jnp.dot(oa_ref[...], wa_ref[...], preferred_element_type=f32)
    b = jnp.dot(on_ref[...], wb_ref[...], preferred_element_type=f32)
    ga = jax.nn.sigmoid(ga_ref[...].astype(f32))
    gb = jax.nn.sigmoid(gb_ref[...].astype(f32))
    merged = (ga * a + gb * b).astype(bf16)
    x1 = x_ref[...] + jnp.dot(merged, wo_ref[...], preferred_element_type=f32)
    x1_ref[...] = x1

    h2 = x1 * lax.rsqrt(jnp.mean(x1 * x1, axis=-1, keepdims=True) + EPS) * g2_ref[...]
    h2_ref[...] = h2
    h_hi = h2.astype(bf16)
    h_lo = (h2 - h_hi.astype(f32)).astype(bf16)
    hw = jnp.dot(h_hi, wr_ref[...], preferred_element_type=f32)
    logits = (hw[:, :LANES] + hw[:, LANES:]
              + jnp.dot(h_lo, wr_ref[:, :LANES], preferred_element_type=f32)) + br_ref[...]

    lane = lax.broadcasted_iota(i32, logits.shape, 1)
    gl = jnp.where(lane < N_GROUPS, logits, NEG_INF)
    gmax = jnp.max(gl, axis=-1, keepdims=True)
    gsel = jnp.min(jnp.where(gl == gmax, lane, LANES), axis=-1, keepdims=True)
    gw = 1.0 / jnp.sum(jnp.exp(gl - gmax), axis=-1, keepdims=True)
    lo = ROUTE_LANE0 + gsel * EXPERTS_PER_GROUP
    el = jnp.where((lane >= lo) & (lane < lo + EXPERTS_PER_GROUP), logits, NEG_INF)
    m1 = jnp.max(el, axis=-1, keepdims=True)
    i1 = jnp.min(jnp.where(el == m1, lane, LANES), axis=-1, keepdims=True)
    el2 = jnp.where(lane == i1, NEG_INF, el)
    m2 = jnp.max(el2, axis=-1, keepdims=True)
    i2 = jnp.min(jnp.where(el2 == m2, lane, LANES), axis=-1, keepdims=True)
    r = jnp.exp(m2 - m1)
    gate1 = gw / (1.0 + r)
    gate2 = gw * r / (1.0 + r)

    oh = ((lane == i1) | (lane == i2)).astype(bf16)
    row = lax.broadcasted_iota(i32, (tm, tm), 0)
    col = lax.broadcasted_iota(i32, (tm, tm), 1)
    tri = (col < row).astype(bf16)
    before = carry_ref[...] + jnp.dot(tri, oh, preferred_element_type=f32)
    r1 = jnp.sum(jnp.where(lane == i1, before, 0.0), axis=-1, keepdims=True).astype(i32)
    r2 = jnp.sum(jnp.where(lane == i2, before, 0.0), axis=-1, keepdims=True).astype(i32)
    carry = carry_ref[...] + jnp.sum(oh.astype(f32), axis=0, keepdims=True)
    carry_ref[...] = carry
    cnt_ref[...] = carry

    ri_ref[...] = jnp.where(lane == 0, i1 - ROUTE_LANE0, jnp.where(lane == 1, i2 - ROUTE_LANE0,
                            jnp.where(lane == 2, r1, jnp.where(lane == 3, r2, 0))))
    rf_ref[...] = jnp.where(lane == 0, gate1, jnp.where(lane == 1, gate2, 0.0))


def _merge(oa, on, proj, x, wa, wb, wo, g2, wr, br, tm=256):
    n, d = x.shape
    tm = min(tm, n)
    aw = oa.shape[1]
    const = lambda shape: pl.BlockSpec(shape, lambda i: (0,) * len(shape), pipeline_mode=pl.Buffered(1))
    return pl.pallas_call(
        _merge_body,
        grid=(n // tm,),
        in_specs=[pl.BlockSpec((tm, aw), lambda i: (i, 0)),
                  pl.BlockSpec((tm, aw), lambda i: (i, 0)),
                  pl.BlockSpec((pl.Element(tm), pl.Element(d)), lambda i: (i * tm, GATE0 * HEAD_DIM)),
                  pl.BlockSpec((pl.Element(tm), pl.Element(d)), lambda i: (i * tm, GATE0 * HEAD_DIM + d)),
                  pl.BlockSpec((tm, d), lambda i: (i, 0)),
                  const((aw, d)), const((aw, d)), const((d, d)), const((1, d)),
                  const((d, 2 * LANES)), const((1, LANES))],
        out_specs=[pl.BlockSpec((tm, d), lambda i: (i, 0)),
                   pl.BlockSpec((tm, d), lambda i: (i, 0)),
                   pl.BlockSpec((tm, LANES), lambda i: (i, 0)),
                   pl.BlockSpec((tm, LANES), lambda i: (i, 0)),
                   pl.BlockSpec((1, LANES), lambda i: (0, 0))],
        out_shape=[jax.ShapeDtypeStruct((n, d), f32), jax.ShapeDtypeStruct((n, d), f32),
                   jax.ShapeDtypeStruct((n, LANES), i32), jax.ShapeDtypeStruct((n, LANES), f32),
                   jax.ShapeDtypeStruct((1, LANES), f32)],
        scratch_shapes=[pltpu.VMEM((1, LANES), f32)],
        compiler_params=_cparams(("arbitrary",)),
        name="merge_route",
    )(oa, on, proj, proj, x, wa, wb, wo, g2, wr, br)


def _row_copy(src_ref, src_row, dst_ref, dst_row, sem):
    return pltpu.make_async_copy(src_ref.at[pl.ds(src_row, 1), :], dst_ref.at[pl.ds(dst_row, 1), :], sem)


def _dispatch_body(d1_ref, d2_ref, pe_ref, pc_ref, nu_ref, h_ref, xbuf_ref, zero_ref, sem, zsem):
    tm = h_ref.shape[0]
    tb = zero_ref.shape[0]
    nb = xbuf_ref.shape[0] // tb
    base = pl.program_id(0) * tm

    def zero_block(row):
        return pltpu.make_async_copy(zero_ref, xbuf_ref.at[pl.ds(pl.multiple_of(row, tb), tb), :], zsem)

    @pl.when(pl.program_id(0) == 0)
    def _():
        zero_ref[...] = jnp.zeros_like(zero_ref)

        def over_blocks(op):
            def tail(e, c):
                @pl.when(pc_ref[e] > 0)
                def _():
                    op(zero_block(pe_ref[e] - tb))
                return c

            def unused(b, c):
                op(zero_block(b * tb))
                return c

            lax.fori_loop(0, N_EXPERTS, tail, 0)
            lax.fori_loop(nu_ref[0], nb, unused, 0)

        over_blocks(lambda cp: cp.start())
        over_blocks(lambda cp: cp.wait())

    def issue(t, c):
        _row_copy(h_ref, t, xbuf_ref, d1_ref[base + t], sem).start()
        _row_copy(h_ref, t, xbuf_ref, d2_ref[base + t], sem).start()
        return c

    def drain(t, c):
        _row_copy(h_ref, t, xbuf_ref, d1_ref[base + t], sem).wait()
        _row_copy(h_ref, t, xbuf_ref, d2_ref[base + t], sem).wait()
        return c

    lax.fori_loop(0, tm, issue, 0, unroll=DMA_UNROLL)
    lax.fori_loop(0, tm, drain, 0, unroll=DMA_UNROLL)


def _dispatch(d1, d2, pad_end, padded, n_used, h2, n_rows, tm=256):
    n, d = h2.shape
    tm = min(tm, n)
    return pl.pallas_call(
        _dispatch_body,
        grid_spec=pltpu.PrefetchScalarGridSpec(
            num_scalar_prefetch=5,
            grid=(n // tm,),
            in_specs=[pl.BlockSpec((tm, d), lambda i, *_: (i, 0))],
            out_specs=pl.BlockSpec(memory_space=pl.ANY),
            scratch_shapes=[pltpu.VMEM((EXPERT_BLOCK, d), f32), pltpu.SemaphoreType.DMA(()),
                            pltpu.SemaphoreType.DMA(())]),
        out_shape=jax.ShapeDtypeStruct((n_rows, d), f32),
        compiler_params=_cparams(("arbitrary",)),
        name="moe_dispatch",
    )(d1, d2, pad_end, padded, n_used, h2)


def _expert_body(be_ref, nu_ref, x_ref, wi_ref, wd_ref, y_ref):
    @pl.when(pl.program_id(0) < nu_ref[0])
    def _():
        h = jnp.dot(x_ref[...].astype(bf16), wi_ref[0], preferred_element_type=f32)
        a, u = h[:, :D_EXPERT], h[:, D_EXPERT:]
        act = (a * jax.nn.sigmoid(a) * u).astype(bf16)
        y_ref[...] = jnp.dot(act, wd_ref[0], preferred_element_type=f32)

    @pl.when(pl.program_id(0) >= nu_ref[0])
    def _():
        y_ref[...] = jnp.zeros_like(y_ref)


def _experts(blk_expert, n_used, xbuf, wi, wd):
    n_rows, d = xbuf.shape
    tb = EXPERT_BLOCK
    return pl.pallas_call(
        _expert_body,
        grid_spec=pltpu.PrefetchScalarGridSpec(
            num_scalar_prefetch=2,
            grid=(n_rows // tb,),
            in_specs=[pl.BlockSpec((tb, d), lambda b, be, nu: (b, 0)),
                      pl.BlockSpec((1, d, 2 * D_EXPERT), lambda b, be, nu: (be[b], 0, 0)),
                      pl.BlockSpec((1, D_EXPERT, d), lambda b, be, nu: (be[b], 0, 0))],
            out_specs=pl.BlockSpec((tb, d), lambda b, be, nu: (b, 0))),
        out_shape=jax.ShapeDtypeStruct((n_rows, d), f32),
        compiler_params=_cparams(("arbitrary",)),
        name="moe_experts",
    )(blk_expert, n_used, xbuf, wi, wd)


def _combine_body(d1_ref, d2_ref, y_ref, x1_ref, rf_ref, gf_ref, o_ref, y1_buf, y2_buf, sem):
    tm = x1_ref.shape[0]
    base = pl.program_id(0) * tm

    def issue(t, c):
        _row_copy(y_ref, d1_ref[base + t], y1_buf, t, sem).start()
        _row_copy(y_ref, d2_ref[base + t], y2_buf, t, sem).start()
        return c

    def drain(t, c):
        _row_copy(y_ref, d1_ref[base + t], y1_buf, t, sem).wait()
        _row_copy(y_ref, d2_ref[base + t], y2_buf, t, sem).wait()
        return c

    lax.fori_loop(0, tm, issue, 0, unroll=DMA_UNROLL)
    lax.fori_loop(0, tm, drain, 0, unroll=DMA_UNROLL)
    rf = rf_ref[...]
    y = y1_buf[...] * rf[:, 0:1] + y2_buf[...] * rf[:, 1:2]
    x = x1_ref[...] + y
    o_ref[...] = x * lax.rsqrt(jnp.mean(x * x, axis=-1, keepdims=True) + EPS) * gf_ref[...]


def _combine(d1, d2, ybuf, x1, rf, gf, tm=256):
    n, d = x1.shape
    tm = min(tm, n)
    return pl.pallas_call(
        _combine_body,
        grid_spec=pltpu.PrefetchScalarGridSpec(
            num_scalar_prefetch=2,
            grid=(n // tm,),
            in_specs=[pl.BlockSpec(memory_space=pl.ANY),
                      pl.BlockSpec((tm, d), lambda i, d1, d2: (i, 0)),
                      pl.BlockSpec((tm, LANES), lambda i, d1, d2: (i, 0)),
                      pl.BlockSpec((1, d), lambda i, d1, d2: (0, 0))],
            out_specs=pl.BlockSpec((tm, d), lambda i, d1, d2: (i, 0)),
            scratch_shapes=[pltpu.VMEM((tm, d), f32), pltpu.VMEM((tm, d), f32), pltpu.SemaphoreType.DMA(())]),
        out_shape=jax.ShapeDtypeStruct((n, d), f32),
        compiler_params=_cparams(("arbitrary",)),
        name="moe_combine",
    )(d1, d2, ybuf, x1, rf, gf)


def _rope_tables(seq):
    t = jnp.arange(seq, dtype=i32)
    pos = jnp.stack([t // GRID_W, t % GRID_W], axis=-1).astype(f32)
    half = HEAD_DIM // 2
    inv_freq = ROPE_THETA ** (-jnp.arange(0, half, 2, dtype=f32) / half)
    ang = pos[:, :, None] * inv_freq[None, None, :]
    ang = jnp.concatenate([ang, ang], axis=-1).reshape(seq, HEAD_DIM)
    lane = jnp.arange(HEAD_DIM, dtype=i32)
    sign = jnp.where((lane % half) < half // 2, -1.0, 1.0).astype(f32)
    return jnp.cos(ang), jnp.sin(ang) * sign


def _layer(x, p):
    batch, seq, d = x.shape
    n = batch * seq
    x2 = x.reshape(n, d)
    proj = _inproj(x2, p["norm1_g"], p["w_in"])
    cos, sin = _rope_tables(seq)
    q, k = _qkprep(proj, cos, sin, p["q_norm_g"], p["k_norm_g"], seq)
    oa = _gqa(q, k, proj, batch, seq)
    on = _na(proj, _na_bias_tables(p["na_rpb"], seq // GRID_W), batch, seq)
    x1, h2, ri, rf, cnt = _merge(oa, on, proj, x2, p["w_branch_a"], p["w_branch_b"], p["w_out"], p["norm2_g"],
                                 p["w_router"], p["b_router"])

    tb = EXPERT_BLOCK
    n_rows = 2 * n + N_EXPERTS * tb
    counts = cnt[0, ROUTE_LANE0:ROUTE_LANE0 + N_EXPERTS].astype(i32)
    padded = (counts + tb - 1) // tb * tb
    pad_end = jnp.cumsum(padded)
    pad_start = pad_end - padded
    eids = jnp.arange(N_EXPERTS, dtype=i32)

    def start_of(e):
        return jnp.sum(jnp.where(e[:, None] == eids[None, :], pad_start[None, :], 0), axis=1)

    d1 = start_of(ri[:, 0]) + ri[:, 2]
    d2 = start_of(ri[:, 1]) + ri[:, 3]
    nb = n_rows // tb
    blk_start = jnp.arange(nb, dtype=i32) * tb
    blk_expert = jnp.minimum(jnp.sum((pad_end[None, :] <= blk_start[:, None]).astype(i32), axis=1), N_EXPERTS - 1)
    n_used = (pad_end[-1:] // tb).astype(i32)

    xbuf = _dispatch(d1, d2, pad_end, padded, n_used, h2, n_rows)
    ybuf = _experts(blk_expert, n_used, xbuf, p["w_exp_in"], p["w_exp_down"])
    out = _combine(d1, d2, ybuf, x1, rf, p["norm_f_g"])
    return out.reshape(batch, seq, d)


def kernel(x_prompt, x_sample, norm1_g, w_in, q_norm_g, k_norm_g, na_rpb, w_branch_a, w_branch_b, w_out, norm2_g,
           w_router_group, b_router_group, w_router_expert, b_router_expert, w_exp_in, w_exp_down, norm_f_g):
    assert norm1_g.shape[0] == 1, "one encoder layer"
    w_r = jnp.zeros((D_MODEL, LANES), f32)
    w_r = w_r.at[:, :N_GROUPS].set(w_router_group[0]).at[:, ROUTE_LANE0:ROUTE_LANE0 + N_EXPERTS].set(w_router_expert[0])
    b_r = jnp.zeros((1, LANES), f32)
    b_r = b_r.at[0, :N_GROUPS].set(b_router_group[0]).at[0, ROUTE_LANE0:ROUTE_LANE0 + N_EXPERTS].set(b_router_expert[0])
    w_r_hi = w_r.astype(bf16)
    p = {
        "norm1_g": norm1_g[0][None], "w_in": w_in[0].astype(bf16),
        "q_norm_g": q_norm_g[0][None], "k_norm_g": k_norm_g[0][None], "na_rpb": na_rpb[0],
        "w_branch_a": w_branch_a[0].astype(bf16), "w_branch_b": w_branch_b[0].astype(bf16),
        "w_out": w_out[0].astype(bf16), "norm2_g": norm2_g[0][None],
        "w_router": jnp.concatenate([w_r_hi, (w_r - w_r_hi.astype(f32)).astype(bf16)], axis=1), "b_router": b_r,
        "w_exp_in": w_exp_in[0].astype(bf16), "w_exp_down": w_exp_down[0].astype(bf16),
        "norm_f_g": norm_f_g[None],
    }
    return _layer(x_prompt, p), _layer(x_sample, p)
```

```python
import functools

import jax
import jax.numpy as jnp
from jax import lax
from jax.experimental import pallas as pl
from jax.experimental.pallas import tpu as pltpu

f32 = jnp.float32
bf16 = jnp.bfloat16
i32 = jnp.int32

D_MODEL = 2048
GRID_W = 64
HEAD_DIM = 128
A_HEADS = 8
A_KV_HEADS = 2
A_GROUP = A_HEADS // A_KV_HEADS
NA_HEADS = 8
NA_WIN_ROWS = 8
NA_WIN_COLS = 16
Q_BLOCK = 128
ROPE_THETA = 10000.0
N_GROUPS = 4
EXPERTS_PER_GROUP = 8
N_EXPERTS = N_GROUPS * EXPERTS_PER_GROUP
D_EXPERT = D_MODEL // 4
EPS = 1e-6
NEG_INF = -1e30
SCALE = HEAD_DIM ** -0.5
LOG2E = 1.4426950408889634

QA0, KA0, VA0, QN0, KN0, VN0, GATE0 = 0, 8, 10, 12, 20, 28, 36
IN_COLS = (GATE0 + 2 * D_MODEL // HEAD_DIM) * HEAD_DIM

LANES = 128
NA_BAND_ROWS = 10
NA_KEYS = NA_BAND_ROWS * GRID_W
ROUTE_LANE0 = 4
EXPERT_BLOCK = 256
DMA_UNROLL = 8
VMEM_LIMIT = 56 * 1024 * 1024


def _cparams(sem):
    return pltpu.CompilerParams(dimension_semantics=sem, vmem_limit_bytes=VMEM_LIMIT)


def _inproj_body(x_ref, g_ref, w_ref, o_ref, h_ref):
    @pl.when(pl.program_id(1) == 0)
    def _():
        x = x_ref[...]
        ms = jnp.mean(x * x, axis=-1, keepdims=True)
        h_ref[...] = (x * lax.rsqrt(ms + EPS) * g_ref[...]).astype(bf16)

    o_ref[...] = jnp.dot(h_ref[...], w_ref[...], preferred_element_type=f32).astype(o_ref.dtype)


def _inproj(x, g, w, tm=1024, tn=IN_COLS // 4):
    n, d = x.shape
    c = w.shape[1]
    tm = min(tm, n)
    return pl.pallas_call(
        _inproj_body,
        grid=(n // tm, c // tn),
        in_specs=[pl.BlockSpec((tm, d), lambda i, j: (i, 0)),
                  pl.BlockSpec((1, d), lambda i, j: (0, 0)),
                  pl.BlockSpec((d, tn), lambda i, j: (0, j))],
        out_specs=pl.BlockSpec((tm, tn), lambda i, j: (i, j)),
        out_shape=jax.ShapeDtypeStruct((n, c), bf16),
        scratch_shapes=[pltpu.VMEM((tm, d), bf16)],
        compiler_params=_cparams(("parallel", "arbitrary")),
        name="inproj",
    )(x, g, w)


def _qkprep_body(q_ref, k_ref, cos_ref, sin_ref, qg_ref, kg_ref, qo_ref, ko_ref):
    cos = cos_ref[...]
    sin = sin_ref[...]
    lane = lax.broadcasted_iota(i32, cos.shape, 1)
    first = (lane % (HEAD_DIM // 2)) < (HEAD_DIM // 4)

    def prep(x, g, scale):
        x = x.astype(f32)
        y = x * lax.rsqrt(jnp.mean(x * x, axis=-1, keepdims=True) + EPS) * g
        rot = jnp.where(first, pltpu.roll(y, HEAD_DIM - HEAD_DIM // 4, 1), pltpu.roll(y, HEAD_DIM // 4, 1))
        return (y * cos + rot * sin) * scale

    for h in range(A_HEADS):
        sl = slice(h * HEAD_DIM, (h + 1) * HEAD_DIM)
        qo_ref[:, sl] = prep(q_ref[:, sl], qg_ref[...], SCALE * LOG2E).astype(bf16)
    for h in range(A_KV_HEADS):
        sl = slice(h * HEAD_DIM, (h + 1) * HEAD_DIM)
        ko_ref[:, sl] = prep(k_ref[:, sl], kg_ref[...], 1.0).astype(bf16)


def _qkprep(proj, cos, sin, qg, kg, seq, tm=512):
    n = proj.shape[0]
    tm = min(tm, seq)
    spb = seq // tm
    qw, kw = A_HEADS * HEAD_DIM, A_KV_HEADS * HEAD_DIM
    return pl.pallas_call(
        _qkprep_body,
        grid=(n // tm,),
        in_specs=[pl.BlockSpec((tm, qw), lambda i: (i, QA0 * HEAD_DIM // qw)),
                  pl.BlockSpec((tm, kw), lambda i: (i, KA0 * HEAD_DIM // kw)),
                  pl.BlockSpec((tm, HEAD_DIM), lambda i: (i % spb, 0)),
                  pl.BlockSpec((tm, HEAD_DIM), lambda i: (i % spb, 0)),
                  pl.BlockSpec((1, HEAD_DIM), lambda i: (0, 0)),
                  pl.BlockSpec((1, HEAD_DIM), lambda i: (0, 0))],
        out_specs=[pl.BlockSpec((tm, qw), lambda i: (i, 0)),
                   pl.BlockSpec((tm, kw), lambda i: (i, 0))],
        out_shape=[jax.ShapeDtypeStruct((n, qw), bf16), jax.ShapeDtypeStruct((n, kw), bf16)],
        compiler_params=_cparams(("parallel",)),
        name="qkprep",
    )(proj, proj, cos, sin, qg, kg)


def _lane_tile(x, n):
    return jnp.concatenate([x] * n, axis=1)


def _gqa_body(q_ref, k_ref, v_ref, o_ref, m_ref, acc_ref, *, tk):
    nkv = k_ref.shape[0] // tk
    m_ref[...] = jnp.full(m_ref.shape, NEG_INF, f32)
    acc_ref[...] = jnp.zeros(acc_ref.shape, f32)
    ones = jnp.ones((tk, HEAD_DIM), bf16)

    def step(j, c):
        off = pl.multiple_of(j * tk, tk)
        k = k_ref[pl.ds(off, tk), :]
        v1 = jnp.concatenate([v_ref[pl.ds(off, tk), :], ones], axis=1)

        def scores(g):
            q = q_ref[:, g * HEAD_DIM:(g + 1) * HEAD_DIM]
            return lax.dot_general(q, k, (((1,), (1,)), ((), ())), preferred_element_type=f32)

        s = scores(0)
        for g in range(A_GROUP):
            s_next = scores(g + 1) if g + 1 < A_GROUP else None
            m = m_ref[g]
            m_new = jnp.maximum(m, jnp.max(s, axis=-1, keepdims=True))
            alpha = jnp.exp2(m - m_new)
            p = jnp.exp2(s - _lane_tile(m_new, tk // LANES))
            pv = jnp.dot(p.astype(bf16), v1, preferred_element_type=f32)
            acc_ref[g] = _lane_tile(alpha, 2) * acc_ref[g] + pv
            m_ref[g] = m_new
            s = s_next
        return c

    lax.fori_loop(0, nkv, step, 0)
    for g in range(A_GROUP):
        acc = acc_ref[g]
        o_ref[:, g * HEAD_DIM:(g + 1) * HEAD_DIM] = (acc[:, :HEAD_DIM] / acc[:, HEAD_DIM:]).astype(o_ref.dtype)


def _gqa(q, k, proj, batch, seq, tq=1024, tk=2048):
    n = q.shape[0]
    tq, tk = min(tq, seq), min(tk, seq)
    nq = seq // tq
    gw = A_GROUP * HEAD_DIM
    return pl.pallas_call(
        functools.partial(_gqa_body, tk=tk),
        grid=(batch, A_KV_HEADS, nq),
        in_specs=[pl.BlockSpec((tq, gw), lambda b, h, i: (b * nq + i, h)),
                  pl.BlockSpec((seq, HEAD_DIM), lambda b, h, i: (b, h)),
                  pl.BlockSpec((seq, HEAD_DIM), lambda b, h, i: (b, VA0 + h))],
        out_specs=pl.BlockSpec((tq, gw), lambda b, h, i: (b * nq + i, h)),
        out_shape=jax.ShapeDtypeStruct((n, A_HEADS * HEAD_DIM), bf16),
        scratch_shapes=[pltpu.VMEM((A_GROUP, tq, LANES), f32), pltpu.VMEM((A_GROUP, tq, 2 * HEAD_DIM), f32)],
        compiler_params=_cparams(("parallel", "parallel", "arbitrary")),
        name="gqa",
    )(q, k, proj)


def _na_bias_tables(rpb, rows):
    nblk = rows * GRID_W // Q_BLOCK
    rpq = Q_BLOCK // GRID_W
    ncls = 5
    blk = jnp.array([0, 1, 2, nblk - 2, nblk - 1], i32)
    col = jnp.arange(GRID_W, dtype=i32)
    col_start = jnp.clip(col - NA_WIN_COLS // 2, 0, GRID_W - NA_WIN_COLS)
    col_in = (col[None, :] >= col_start[:, None]) & (col[None, :] < col_start[:, None] + NA_WIN_COLS)
    dc = jnp.clip(col[None, :] - col[:, None] + NA_WIN_COLS - 1, 0, 2 * NA_WIN_COLS - 2)
    r0 = blk * rpq
    bs = jnp.clip(r0 - NA_WIN_ROWS // 2, 0, rows - NA_BAND_ROWS)
    q_row = r0[:, None] + jnp.arange(rpq, dtype=i32)[None, :]
    k_row = bs[:, None] + jnp.arange(NA_BAND_ROWS, dtype=i32)[None, :]
    row_start = jnp.clip(q_row - NA_WIN_ROWS // 2, 0, rows - NA_WIN_ROWS)
    row_in = (k_row[:, None, :] >= row_start[:, :, None]) & (k_row[:, None, :] < row_start[:, :, None] + NA_WIN_ROWS)
    dr = jnp.clip(k_row[:, None, :] - q_row[:, :, None] + NA_WIN_ROWS - 1, 0, 2 * NA_WIN_ROWS - 2)
    picked = rpb.astype(f32)[:, dr, :]
    onehot = (dc[None] == jnp.arange(2 * NA_WIN_COLS - 1, dtype=i32)[:, None, None]).astype(f32)
    bias = jnp.einsum("hcqkd,dxy->chqxky", picked, onehot, precision=lax.Precision.HIGHEST)
    mask = row_in[:, None, :, None, :, None] & col_in[None, None, None, :, None, :]
    bias = jnp.where(mask, bias * LOG2E, NEG_INF)
    return bias.reshape(ncls, NA_HEADS, Q_BLOCK, NA_KEYS)


def _na_body(q_ref, k_ref, v_ref, tab_ref, o_ref):
    nh = q_ref.shape[1] // HEAD_DIM
    ones = jnp.ones((NA_KEYS, HEAD_DIM), bf16)

    def scores(h):
        sl = slice(h * HEAD_DIM, (h + 1) * HEAD_DIM)
        s = lax.dot_general(q_ref[:, sl], k_ref[:, sl], (((1,), (1,)), ((), ())), preferred_element_type=f32)
        return s * (SCALE * LOG2E) + tab_ref[0, h]

    s = scores(0)
    for h in range(nh):
        sl = slice(h * HEAD_DIM, (h + 1) * HEAD_DIM)
        s_next = scores(h + 1) if h + 1 < nh else None
        m = jnp.broadcast_to(jnp.max(s, axis=-1, keepdims=True), (Q_BLOCK, LANES))
        p = jnp.exp2(s - _lane_tile(m, NA_KEYS // LANES))
        v1 = jnp.concatenate([v_ref[:, sl], ones], axis=1)
        pv = jnp.dot(p.astype(bf16), v1, preferred_element_type=f32)
        o_ref[:, sl] = (pv[:, :HEAD_DIM] / pv[:, HEAD_DIM:]).astype(o_ref.dtype)
        s = s_next


def _na(proj, tab, batch, seq):
    n = proj.shape[0]
    nblk = seq // Q_BLOCK
    rows = seq // GRID_W
    rpq = Q_BLOCK // GRID_W
    w = NA_HEADS * HEAD_DIM

    def band(b, i):
        first_row = jnp.clip(i * rpq - NA_WIN_ROWS // 2, 0, rows - NA_BAND_ROWS)
        return pl.multiple_of(b * seq + first_row * GRID_W, GRID_W)

    def cls(i):
        return jnp.where(i < 2, i, jnp.where(i <= nblk - 3, 2, i - (nblk - 5)))

    return pl.pallas_call(
        _na_body,
        grid=(batch, nblk),
        in_specs=[pl.BlockSpec((pl.Element(Q_BLOCK), pl.Element(w)),
                               lambda b, i: ((b * nblk + i) * Q_BLOCK, QN0 * HEAD_DIM)),
                  pl.BlockSpec((pl.Element(NA_KEYS), pl.Element(w)), lambda b, i: (band(b, i), KN0 * HEAD_DIM)),
                  pl.BlockSpec((pl.Element(NA_KEYS), pl.Element(w)), lambda b, i: (band(b, i), VN0 * HEAD_DIM)),
                  pl.BlockSpec((1, NA_HEADS, Q_BLOCK, NA_KEYS), lambda b, i: (cls(i), 0, 0, 0))],
        out_specs=pl.BlockSpec((Q_BLOCK, w), lambda b, i: (b * nblk + i, 0)),
        out_shape=jax.ShapeDtypeStruct((n, w), bf16),
        compiler_params=_cparams(("parallel", "arbitrary")),
        name="natten",
    )(proj, proj, proj, tab)


def _merge_body(oa_ref, on_ref, ga_ref, gb_ref, x_ref, wa_ref, wb_ref, wo_ref, g2_ref, wr_ref, br_ref,
                x1_ref, h2_ref, ri_ref, rf_ref, cnt_ref, carry_ref):
    tm = x_ref.shape[0]

    @pl.when(pl.program_id(0) == 0)
    def _():
        carry_ref[...] = jnp.zeros_like(carry_ref)

    a = jnp.dot(oa_ref[...], wa_ref[...], preferred_element_type=f32)
    b = jnp.dot(on_ref[...], wb_ref[...], preferred_element_type=f32)
    ga = jax.nn.sigmoid(ga_ref[...].astype(f32))
    gb = jax.nn.sigmoid(gb_ref[...].astype(f32))
    merged = (ga * a + gb * b).astype(bf16)
    x1 = x_ref[...] + jnp.dot(merged, wo_ref[...], preferred_element_type=f32)
    x1_ref[...] = x1

    h2 = x1 * lax.rsqrt(jnp.mean(x1 * x1, axis=-1, keepdims=True) + EPS) * g2_ref[...]
    h2_ref[...] = h2
    h_hi = h2.astype(bf16)
    h_lo = (h2 - h_hi.astype(f32)).astype(bf16)
    hw = jnp.dot(h_hi, wr_ref[...], preferred_element_type=f32)
    logits = (hw[:, :LANES] + hw[:, LANES:]
              + jnp.dot(h_lo, wr_ref[:, :LANES], preferred_element_type=f32)) + br_ref[...]

    lane = lax.broadcasted_iota(i32, logits.shape, 1)
    gl = jnp.where(lane < N_GROUPS, logits, NEG_INF)
    gmax = jnp.max(gl, axis=-1, keepdims=True)
    gsel = jnp.min(jnp.where(gl == gmax, lane, LANES), axis=-1, keepdims=True)
    gw = 1.0 / jnp.sum(jnp.exp(gl - gmax), axis=-1, keepdims=True)
    lo = ROUTE_LANE0 + gsel * EXPERTS_PER_GROUP
    el = jnp.where((lane >= lo) & (lane < lo + EXPERTS_PER_GROUP), logits, NEG_INF)
    m1 = jnp.max(el, axis=-1, keepdims=True)
    i1 = jnp.min(jnp.where(el == m1, lane, LANES), axis=-1, keepdims=True)
    el2 = jnp.where(lane == i1, NEG_INF, el)
    m2 = jnp.max(el2, axis=-1, keepdims=True)
    i2 = jnp.min(jnp.where(el2 == m2, lane, LANES), axis=-1, keepdims=True)
    r = jnp.exp(m2 - m1)
    gate1 = gw / (1.0 + r)
    gate2 = gw * r / (1.0 + r)

    oh = ((lane == i1) | (lane == i2)).astype(bf16)
    row = lax.broadcasted_iota(i32, (tm, tm), 0)
    col = lax.broadcasted_iota(i32, (tm, tm), 1)
    tri = (col < row).astype(bf16)
    before = carry_ref[...] + jnp.dot(tri, oh, preferred_element_type=f32)
    r1 = jnp.sum(jnp.where(lane == i1, before, 0.0), axis=-1, keepdims=True).astype(i32)
    r2 = jnp.sum(jnp.where(lane == i2, before, 0.0), axis=-1, keepdims=True).astype(i32)
    carry = carry_ref[...] + jnp.sum(oh.astype(f32), axis=0, keepdims=True)
    carry_ref[...] = carry
    cnt_ref[...] = carry

    ri_ref[...] = jnp.where(lane == 0, i1 - ROUTE_LANE0, jnp.where(lane == 1, i2 - ROUTE_LANE0,
                            jnp.where(lane == 2, r1, jnp.where(lane == 3, r2, 0))))
    rf_ref[...] = jnp.where(lane == 0, gate1, jnp.where(lane == 1, gate2, 0.0))


def _merge(oa, on, proj, x, wa, wb, wo, g2, wr, br, tm=256):
    n, d = x.shape
    tm = min(tm, n)
    aw = oa.shape[1]
    const = lambda shape: pl.BlockSpec(shape, lambda i: (0,) * len(shape), pipeline_mode=pl.Buffered(1))
    return pl.pallas_call(
        _merge_body,
        grid=(n // tm,),
        in_specs=[pl.BlockSpec((tm, aw), lambda i: (i, 0)),
                  pl.BlockSpec((tm, aw), lambda i: (i, 0)),
                  pl.BlockSpec((pl.Element(tm), pl.Element(d)), lambda i: (i * tm, GATE0 * HEAD_DIM)),
                  pl.BlockSpec((pl.Element(tm), pl.Element(d)), lambda i: (i * tm, GATE0 * HEAD_DIM + d)),
                  pl.BlockSpec((tm, d), lambda i: (i, 0)),
                  const((aw, d)), const((aw, d)), const((d, d)), const((1, d)),
                  const((d, 2 * LANES)), const((1, LANES))],
        out_specs=[pl.BlockSpec((tm, d), lambda i: (i, 0)),
                   pl.BlockSpec((tm, d), lambda i: (i, 0)),
                   pl.BlockSpec((tm, LANES), lambda i: (i, 0)),
                   pl.BlockSpec((tm, LANES), lambda i: (i, 0)),
                   pl.BlockSpec((1, LANES), lambda i: (0, 0))],
        out_shape=[jax.ShapeDtypeStruct((n, d), f32), jax.ShapeDtypeStruct((n, d), f32),
                   jax.ShapeDtypeStruct((n, LANES), i32), jax.ShapeDtypeStruct((n, LANES), f32),
                   jax.ShapeDtypeStruct((1, LANES), f32)],
        scratch_shapes=[pltpu.VMEM((1, LANES), f32)],
        compiler_params=_cparams(("arbitrary",)),
        name="merge_route",
    )(oa, on, proj, proj, x, wa, wb, wo, g2, wr, br)


def _moe_body(d1_ref, d2_ref, be_ref, bv_ref, nu_ref, h_hbm, wi_ref, wd_ref, y_hbm,
              xin0, xin1, yo0, yo1, wi_bf, wd_bf, enc_ref, gsem, ssem):
    b = pl.program_id(0)
    nu = nu_ref[0]
    n = d1_ref.shape[0]
    tb = xin0.shape[0]
    nb = be_ref.shape[0]
    xin, yo = (xin0, xin1), (yo0, yo1)

    def gather(blk, r, slot):
        enc = enc_ref[blk * tb + r]
        return pltpu.make_async_copy(h_hbm.at[pl.ds(enc & (n - 1), 1), :], xin[slot].at[pl.ds(r, 1), :],
                                     gsem.at[slot])

    def scatter(blk, r, slot):
        enc = enc_ref[blk * tb + r]
        return pltpu.make_async_copy(yo[slot].at[pl.ds(r, 1), :], y_hbm.at[pl.ds(enc, 1), :], ssem.at[slot])

    def for_rows(blk, fn):
        cnt = bv_ref[blk]
        groups = cnt // DMA_UNROLL

        def unrolled(g, c):
            for u in range(DMA_UNROLL):
                fn(g * DMA_UNROLL + u)
            return c

        def single(r, c):
            fn(r)
            return c

        lax.fori_loop(0, groups, unrolled, 0)
        lax.fori_loop(groups * DMA_UNROLL, cnt, single, 0)

    def for_rows_static(blk, fn):
        cnt = bv_ref[blk]
        for r in range(tb):
            pl.when(r < cnt)(functools.partial(fn, r))

    @pl.when(b == 0)
    def _():
        xin0[...] = jnp.zeros_like(xin0)
        xin1[...] = jnp.zeros_like(xin1)

        def fill(t, c):
            enc_ref[d1_ref[t]] = t
            enc_ref[d2_ref[t]] = n + t
            return c

        lax.fori_loop(0, n, fill, 0, unroll=DMA_UNROLL)
        for_rows(0, lambda r: gather(0, r, 0).start())

    def step(p):
        nxt = jnp.minimum(b + 1, nb - 1)

        @pl.when(b >= 2)
        def _():
            for_rows(b - 2, lambda r: scatter(b - 2, r, p).wait())

        for_rows(b, lambda r: gather(b, r, p).wait())
        for_rows_static(nxt, lambda r: gather(nxt, r, 1 - p).start())
        h = jnp.dot(xin[p][...].astype(bf16), wi_bf[...], preferred_element_type=f32)
        a, u = h[:, :D_EXPERT], h[:, D_EXPERT:]
        act = (a * jax.nn.sigmoid(a) * u).astype(bf16)
        yo[p][...] = jnp.dot(act, wd_bf[...], preferred_element_type=f32)
        for_rows_static(b, lambda r: scatter(b, r, p).start())

        @pl.when(b == nu - 1)
        def _():
            for_rows(nxt, lambda r: gather(nxt, r, 1 - p).wait())
            for_rows(b, lambda r: scatter(b, r, p).wait())

            @pl.when(b >= 1)
            def _():
                for_rows(b - 1, lambda r: scatter(b - 1, r, 1 - p).wait())

    @pl.when((b < nu) & ((b == 0) | (be_ref[b] != be_ref[jnp.maximum(b - 1, 0)])))
    def _():
        wi_bf[...] = wi_ref[0].astype(bf16)
        wd_bf[...] = wd_ref[0].astype(bf16)

    for p in range(2):
        pl.when((b < nu) & (b % 2 == p))(functools.partial(step, p))


def _moe(d1, d2, blk_expert, blk_valid, n_used, h2, wi, wd):
    n, d = h2.shape
    assert n & (n - 1) == 0, "token count must be a power of two (row index = enc & (n - 1))"
    tb = EXPERT_BLOCK
    nb = blk_expert.shape[0]
    return pl.pallas_call(
        _moe_body,
        grid_spec=pltpu.PrefetchScalarGridSpec(
            num_scalar_prefetch=5,
            grid=(nb,),
            in_specs=[pl.BlockSpec(memory_space=pl.ANY),
                      pl.BlockSpec((1, d, 2 * D_EXPERT), lambda b, d1, d2, be, bv, nu: (be[b], 0, 0)),
                      pl.BlockSpec((1, D_EXPERT, d), lambda b, d1, d2, be, bv, nu: (be[b], 0, 0))],
            out_specs=pl.BlockSpec(memory_space=pl.ANY),
            scratch_shapes=[pltpu.VMEM((tb, d), f32), pltpu.VMEM((tb, d), f32),
                            pltpu.VMEM((tb, d), f32), pltpu.VMEM((tb, d), f32),
                            pltpu.VMEM((d, 2 * D_EXPERT), bf16), pltpu.VMEM((D_EXPERT, d), bf16),
                            pltpu.SMEM((nb * tb,), i32),
                            pltpu.SemaphoreType.DMA((2,)), pltpu.SemaphoreType.DMA((2,))]),
        out_shape=jax.ShapeDtypeStruct((2 * n, d), f32),
        compiler_params=_cparams(("arbitrary",)),
        name="moe_experts",
    )(d1, d2, blk_expert, blk_valid, n_used, h2, wi, wd)


def _final_body(y1_ref, y2_ref, x1_ref, rf_ref, gf_ref, o_ref):
    rf = rf_ref[...]
    x = x1_ref[...] + (y1_ref[...] * rf[:, 0:1] + y2_ref[...] * rf[:, 1:2])
    o_ref[...] = x * lax.rsqrt(jnp.mean(x * x, axis=-1, keepdims=True) + EPS) * gf_ref[...]


def _final(y, x1, rf, gf, tm=512):
    n, d = x1.shape
    tm = min(tm, n)
    return pl.pallas_call(
        _final_body,
        grid=(n // tm,),
        in_specs=[pl.BlockSpec((tm, d), lambda i: (i, 0)),
                  pl.BlockSpec((tm, d), lambda i: (i + n // tm, 0)),
                  pl.BlockSpec((tm, d), lambda i: (i, 0)),
                  pl.BlockSpec((tm, LANES), lambda i: (i, 0)),
                  pl.BlockSpec((1, d), lambda i: (0, 0))],
        out_specs=pl.BlockSpec((tm, d), lambda i: (i, 0)),
        out_shape=jax.ShapeDtypeStruct((n, d), f32),
        compiler_params=_cparams(("parallel",)),
        name="moe_final",
    )(y, y, x1, rf, gf)


def _rope_tables(seq):
    t = jnp.arange(seq, dtype=i32)
    pos = jnp.stack([t // GRID_W, t % GRID_W], axis=-1).astype(f32)
    half = HEAD_DIM // 2
    inv_freq = ROPE_THETA ** (-jnp.arange(0, half, 2, dtype=f32) / half)
    ang = pos[:, :, None] * inv_freq[None, None, :]
    ang = jnp.concatenate([ang, ang], axis=-1).reshape(seq, HEAD_DIM)
    lane = jnp.arange(HEAD_DIM, dtype=i32)
    sign = jnp.where((lane % half) < half // 2, -1.0, 1.0).astype(f32)
    return jnp.cos(ang), jnp.sin(ang) * sign


def _layer(x, p):
    batch, seq, d = x.shape
    n = batch * seq
    x2 = x.reshape(n, d)
    proj = _inproj(x2, p["norm1_g"], p["w_in"])
    cos, sin = _rope_tables(seq)
    q, k = _qkprep(proj, cos, sin, p["q_norm_g"], p["k_norm_g"], seq)
    oa = _gqa(q, k, proj, batch, seq)
    on = _na(proj, _na_bias_tables(p["na_rpb"], seq // GRID_W), batch, seq)
    x1, h2, ri, rf, cnt = _merge(oa, on, proj, x2, p["w_branch_a"], p["w_branch_b"], p["w_out"], p["norm2_g"],
                                 p["w_router"], p["b_router"])

    tb = EXPERT_BLOCK
    n_rows = 2 * n + N_EXPERTS * tb
    counts = cnt[0, ROUTE_LANE0:ROUTE_LANE0 + N_EXPERTS].astype(i32)
    padded = (counts + tb - 1) // tb * tb
    pad_end = jnp.cumsum(padded)
    pad_start = pad_end - padded
    eids = jnp.arange(N_EXPERTS, dtype=i32)

    def start_of(e):
        return jnp.sum(jnp.where(e[:, None] == eids[None, :], pad_start[None, :], 0), axis=1)

    d1 = start_of(ri[:, 0]) + ri[:, 2]
    d2 = start_of(ri[:, 1]) + ri[:, 3]
    nb = n_rows // tb
    blk_start = jnp.arange(nb, dtype=i32) * tb
    blk_expert = jnp.minimum(jnp.sum((pad_end[None, :] <= blk_start[:, None]).astype(i32), axis=1), N_EXPERTS - 1)
    of_blk = blk_expert[:, None] == eids[None, :]
    row_in_expert = blk_start - jnp.sum(jnp.where(of_blk, pad_start[None, :], 0), axis=1)
    blk_valid = jnp.clip(jnp.sum(jnp.where(of_blk, counts[None, :], 0), axis=1) - row_in_expert, 0, tb)
    blk_valid = jnp.where(blk_start < pad_end[-1], blk_valid, 0).astype(i32)
    n_used = (pad_end[-1:] // tb).astype(i32)

    y = _moe(d1, d2, blk_expert, blk_valid, n_used, h2, p["w_exp_in"], p["w_exp_down"])
    out = _final(y, x1, rf, p["norm_f_g"])
    return out.reshape(batch, seq, d)


def kernel(x_prompt, x_sample, norm1_g, w_in, q_norm_g, k_norm_g, na_rpb, w_branch_a, w_branch_b, w_out, norm2_g,
           w_router_group, b_router_group, w_router_expert, b_router_expert, w_exp_in, w_exp_down, norm_f_g):
    assert norm1_g.shape[0] == 1, "one encoder layer"
    w_r = jnp.zeros((D_MODEL, LANES), f32)
    w_r = w_r.at[:, :N_GROUPS].set(w_router_group[0]).at[:, ROUTE_LANE0:ROUTE_LANE0 + N_EXPERTS].set(w_router_expert[0])
    b_r = jnp.zeros((1, LANES), f32)
    b_r = b_r.at[0, :N_GROUPS].set(b_router_group[0]).at[0, ROUTE_LANE0:ROUTE_LANE0 + N_EXPERTS].set(b_router_expert[0])
    w_r_hi = w_r.astype(bf16)
    p = {
        "norm1_g": norm1_g[0][None], "w_in": w_in[0].astype(bf16),
        "q_norm_g": q_norm_g[0][None], "k_norm_g": k_norm_g[0][None], "na_rpb": na_rpb[0],
        "w_branch_a": w_branch_a[0].astype(bf16), "w_branch_b": w_branch_b[0].astype(bf16),
        "w_out": w_out[0].astype(bf16), "norm2_g": norm2_g[0][None],
        "w_router": jnp.concatenate([w_r_hi, (w_r - w_r_hi.astype(f32)).astype(bf16)], axis=1), "b_router": b_r,
        "w_exp_in": w_exp_in[0], "w_exp_down": w_exp_down[0],
        "norm_f_g": norm_f_g[None],
    }
    return _layer(x_prompt, p), _layer(x_sample, p)
```

```python
import functools

import jax
import jax.numpy as jnp
from jax import lax
from jax.experimental import pallas as pl
from jax.experimental.pallas import tpu as pltpu

f32 = jnp.float32
bf16 = jnp.bfloat16
i32 = jnp.int32

D_MODEL = 2048
GRID_W = 64
HEAD_DIM = 128
A_HEADS = 8
A_KV_HEADS = 2
A_GROUP = A_HEADS // A_KV_HEADS
NA_HEADS = 8
NA_WIN_ROWS = 8
NA_WIN_COLS = 16
Q_BLOCK = 128
ROPE_THETA = 10000.0
N_GROUPS = 4
EXPERTS_PER_GROUP = 8
N_EXPERTS = N_GROUPS * EXPERTS_PER_GROUP
D_EXPERT = D_MODEL // 4
EPS = 1e-6
NEG_INF = -1e30
SCALE = HEAD_DIM ** -0.5
LOG2E = 1.4426950408889634

QA0, KA0, VA0, QN0, KN0, VN0, GATE0 = 0, 8, 10, 12, 20, 28, 36
IN_COLS = (GATE0 + 2 * D_MODEL // HEAD_DIM) * HEAD_DIM

LANES = 128
NA_BAND_ROWS = 10
NA_KEYS = NA_BAND_ROWS * GRID_W
ROUTE_LANE0 = 4
EXPERT_BLOCK = 256
DMA_UNROLL = 8
VMEM_LIMIT = 56 * 1024 * 1024


def _cparams(sem):
    return pltpu.CompilerParams(dimension_semantics=sem, vmem_limit_bytes=VMEM_LIMIT)


def _inproj_body(x_ref, g_ref, w_ref, o_ref, h_ref):
    @pl.when(pl.program_id(1) == 0)
    def _():
        x = x_ref[...]
        ms = jnp.mean(x * x, axis=-1, keepdims=True)
        h_ref[...] = (x * lax.rsqrt(ms + EPS) * g_ref[...]).astype(bf16)

    o_ref[...] = jnp.dot(h_ref[...], w_ref[...], preferred_element_type=f32).astype(o_ref.dtype)


def _inproj(x, g, w, tm=1024, tn=IN_COLS // 4):
    n, d = x.shape
    c = w.shape[1]
    tm = min(tm, n)
    return pl.pallas_call(
        _inproj_body,
        grid=(n // tm, c // tn),
        in_specs=[pl.BlockSpec((tm, d), lambda i, j: (i, 0)),
                  pl.BlockSpec((1, d), lambda i, j: (0, 0)),
                  pl.BlockSpec((d, tn), lambda i, j: (0, j))],
        out_specs=pl.BlockSpec((tm, tn), lambda i, j: (i, j)),
        out_shape=jax.ShapeDtypeStruct((n, c), bf16),
        scratch_shapes=[pltpu.VMEM((tm, d), bf16)],
        compiler_params=_cparams(("parallel", "arbitrary")),
        name="inproj",
    )(x, g, w)


def _qkprep_body(q_ref, k_ref, cos_ref, sin_ref, qg_ref, kg_ref, qo_ref, ko_ref):
    cos = cos_ref[...]
    sin = sin_ref[...]
    lane = lax.broadcasted_iota(i32, cos.shape, 1)
    first = (lane % (HEAD_DIM // 2)) < (HEAD_DIM // 4)

    def prep(x, g, scale):
        x = x.astype(f32)
        y = x * lax.rsqrt(jnp.mean(x * x, axis=-1, keepdims=True) + EPS) * g
        rot = jnp.where(first, pltpu.roll(y, HEAD_DIM - HEAD_DIM // 4, 1), pltpu.roll(y, HEAD_DIM // 4, 1))
        return (y * cos + rot * sin) * scale

    for h in range(A_HEADS):
        sl = slice(h * HEAD_DIM, (h + 1) * HEAD_DIM)
        qo_ref[:, sl] = prep(q_ref[:, sl], qg_ref[...], SCALE * LOG2E).astype(bf16)
    for h in range(A_KV_HEADS):
        sl = slice(h * HEAD_DIM, (h + 1) * HEAD_DIM)
        ko_ref[:, sl] = prep(k_ref[:, sl], kg_ref[...], 1.0).astype(bf16)


def _qkprep(proj, cos, sin, qg, kg, seq, tm=512):
    n = proj.shape[0]
    tm = min(tm, seq)
    spb = seq // tm
    qw, kw = A_HEADS * HEAD_DIM, A_KV_HEADS * HEAD_DIM
    return pl.pallas_call(
        _qkprep_body,
        grid=(n // tm,),
        in_specs=[pl.BlockSpec((tm, qw), lambda i: (i, QA0 * HEAD_DIM // qw)),
                  pl.BlockSpec((tm, kw), lambda i: (i, KA0 * HEAD_DIM // kw)),
                  pl.BlockSpec((tm, HEAD_DIM), lambda i: (i % spb, 0)),
                  pl.BlockSpec((tm, HEAD_DIM), lambda i: (i % spb, 0)),
                  pl.BlockSpec((1, HEAD_DIM), lambda i: (0, 0)),
                  pl.BlockSpec((1, HEAD_DIM), lambda i: (0, 0))],
        out_specs=[pl.BlockSpec((tm, qw), lambda i: (i, 0)),
                   pl.BlockSpec((tm, kw), lambda i: (i, 0))],
        out_shape=[jax.ShapeDtypeStruct((n, qw), bf16), jax.ShapeDtypeStruct((n, kw), bf16)],
        compiler_params=_cparams(("parallel",)),
        name="qkprep",
    )(proj, proj, cos, sin, qg, kg)


def _lane_tile(x, n):
    return jnp.concatenate([x] * n, axis=1)


def _gqa_body(q_ref, k_ref, v_ref, o_ref, m_ref, acc_ref, *, tk):
    nkv = k_ref.shape[0] // tk
    m_ref[...] = jnp.full(m_ref.shape, NEG_INF, f32)
    acc_ref[...] = jnp.zeros(acc_ref.shape, f32)
    ones = jnp.ones((tk, HEAD_DIM), bf16)

    def step(j, c):
        off = pl.multiple_of(j * tk, tk)
        k = k_ref[pl.ds(off, tk), :]
        v1 = jnp.concatenate([v_ref[pl.ds(off, tk), :], ones], axis=1)

        def scores(g):
            q = q_ref[:, g * HEAD_DIM:(g + 1) * HEAD_DIM]
            return lax.dot_general(q, k, (((1,), (1,)), ((), ())), preferred_element_type=f32)

        s = scores(0)
        for g in range(A_GROUP):
            s_next = scores(g + 1) if g + 1 < A_GROUP else None
            m = m_ref[g]
            m_new = jnp.maximum(m, jnp.max(s, axis=-1, keepdims=True))
            alpha = jnp.exp2(m - m_new)
            p = jnp.exp2(s - _lane_tile(m_new, tk // LANES))
            pv = jnp.dot(p.astype(bf16), v1, preferred_element_type=f32)
            acc_ref[g] = _lane_tile(alpha, 2) * acc_ref[g] + pv
            m_ref[g] = m_new
            s = s_next
        return c

    lax.fori_loop(0, nkv, step, 0)
    for g in range(A_GROUP):
        acc = acc_ref[g]
        o_ref[:, g * HEAD_DIM:(g + 1) * HEAD_DIM] = (acc[:, :HEAD_DIM] / acc[:, HEAD_DIM:]).astype(o_ref.dtype)


def _gqa(q, k, proj, batch, seq, tq=1024, tk=2048):
    n = q.shape[0]
    tq, tk = min(tq, seq), min(tk, seq)
    nq = seq // tq
    gw = A_GROUP * HEAD_DIM
    return pl.pallas_call(
        functools.partial(_gqa_body, tk=tk),
        grid=(batch, A_KV_HEADS, nq),
        in_specs=[pl.BlockSpec((tq, gw), lambda b, h, i: (b * nq + i, h)),
                  pl.BlockSpec((seq, HEAD_DIM), lambda b, h, i: (b, h)),
                  pl.BlockSpec((seq, HEAD_DIM), lambda b, h, i: (b, VA0 + h))],
        out_specs=pl.BlockSpec((tq, gw), lambda b, h, i: (b * nq + i, h)),
        out_shape=jax.ShapeDtypeStruct((n, A_HEADS * HEAD_DIM), bf16),
        scratch_shapes=[pltpu.VMEM((A_GROUP, tq, LANES), f32), pltpu.VMEM((A_GROUP, tq, 2 * HEAD_DIM), f32)],
        compiler_params=_cparams(("parallel", "parallel", "arbitrary")),
        name="gqa",
    )(q, k, proj)


def _na_bias_tables(rpb, rows):
    nblk = rows * GRID_W // Q_BLOCK
    rpq = Q_BLOCK // GRID_W
    ncls = 5
    blk = jnp.array([0, 1, 2, nblk - 2, nblk - 1], i32)
    col = jnp.arange(GRID_W, dtype=i32)
    col_start = jnp.clip(col - NA_WIN_COLS // 2, 0, GRID_W - NA_WIN_COLS)
    col_in = (col[None, :] >= col_start[:, None]) & (col[None, :] < col_start[:, None] + NA_WIN_COLS)
    dc = jnp.clip(col[None, :] - col[:, None] + NA_WIN_COLS - 1, 0, 2 * NA_WIN_COLS - 2)
    r0 = blk * rpq
    bs = jnp.clip(r0 - NA_WIN_ROWS // 2, 0, rows - NA_BAND_ROWS)
    q_row = r0[:, None] + jnp.arange(rpq, dtype=i32)[None, :]
    k_row = bs[:, None] + jnp.arange(NA_BAND_ROWS, dtype=i32)[None, :]
    row_start = jnp.clip(q_row - NA_WIN_ROWS // 2, 0, rows - NA_WIN_ROWS)
    row_in = (k_row[:, None, :] >= row_start[:, :, None]) & (k_row[:, None, :] < row_start[:, :, None] + NA_WIN_ROWS)
    dr = jnp.clip(k_row[:, None, :] - q_row[:, :, None] + NA_WIN_ROWS - 1, 0, 2 * NA_WIN_ROWS - 2)
    picked = rpb.astype(f32)[:, dr, :]
    onehot = (dc[None] == jnp.arange(2 * NA_WIN_COLS - 1, dtype=i32)[:, None, None]).astype(f32)
    bias = jnp.einsum("hcqkd,dxy->chqxky", picked, onehot, precision=lax.Precision.HIGHEST)
    mask = row_in[:, None, :, None, :, None] & col_in[None, None, None, :, None, :]
    bias = jnp.where(mask, bias * LOG2E, NEG_INF)
    return bias.reshape(ncls, NA_HEADS, Q_BLOCK, NA_KEYS)


def _na_body(q_ref, k_ref, v_ref, tab_ref, o_ref):
    nh = q_ref.shape[1] // HEAD_DIM
    ones = jnp.ones((NA_KEYS, HEAD_DIM), bf16)

    def scores(h):
        sl = slice(h * HEAD_DIM, (h + 1) * HEAD_DIM)
        s = lax.dot_general(q_ref[:, sl], k_ref[:, sl], (((1,), (1,)), ((), ())), preferred_element_type=f32)
        return s * (SCALE * LOG2E) + tab_ref[0, h]

    s = scores(0)
    for h in range(nh):
        sl = slice(h * HEAD_DIM, (h + 1) * HEAD_DIM)
        s_next = scores(h + 1) if h + 1 < nh else None
        m = jnp.broadcast_to(jnp.max(s, axis=-1, keepdims=True), (Q_BLOCK, LANES))
        p = jnp.exp2(s - _lane_tile(m, NA_KEYS // LANES))
        v1 = jnp.concatenate([v_ref[:, sl], ones], axis=1)
        pv = jnp.dot(p.astype(bf16), v1, preferred_element_type=f32)
        o_ref[:, sl] = (pv[:, :HEAD_DIM] / pv[:, HEAD_DIM:]).astype(o_ref.dtype)
        s = s_next


def _na(proj, tab, batch, seq):
    n = proj.shape[0]
    nblk = seq // Q_BLOCK
    rows = seq // GRID_W
    rpq = Q_BLOCK // GRID_W
    w = NA_HEADS * HEAD_DIM

    def band(b, i):
        first_row = jnp.clip(i * rpq - NA_WIN_ROWS // 2, 0, rows - NA_BAND_ROWS)
        return pl.multiple_of(b * seq + first_row * GRID_W, GRID_W)

    def cls(i):
        return jnp.where(i < 2, i, jnp.where(i <= nblk - 3, 2, i - (nblk - 5)))

    return pl.pallas_call(
        _na_body,
        grid=(batch, nblk),
        in_specs=[pl.BlockSpec((pl.Element(Q_BLOCK), pl.Element(w)),
                               lambda b, i: ((b * nblk + i) * Q_BLOCK, QN0 * HEAD_DIM)),
                  pl.BlockSpec((pl.Element(NA_KEYS), pl.Element(w)), lambda b, i: (band(b, i), KN0 * HEAD_DIM)),
                  pl.BlockSpec((pl.Element(NA_KEYS), pl.Element(w)), lambda b, i: (band(b, i), VN0 * HEAD_DIM)),
                  pl.BlockSpec((1, NA_HEADS, Q_BLOCK, NA_KEYS), lambda b, i: (cls(i), 0, 0, 0))],
        out_specs=pl.BlockSpec((Q_BLOCK, w), lambda b, i: (b * nblk + i, 0)),
        out_shape=jax.ShapeDtypeStruct((n, w), bf16),
        compiler_params=_cparams(("parallel", "arbitrary")),
        name="natten",
    )(proj, proj, proj, tab)


def _pack_bf16_pairs(x):
    c = x.shape[1] // 2
    bits = lax.bitcast_convert_type(x.astype(f32), i32)
    return lax.shift_right_logical(bits[:, :c], 16) | bits[:, c:]


def _unpack_bf16_pairs(packed):
    lo = lax.bitcast_convert_type(lax.shift_left(packed, 16), f32)
    hi = lax.bitcast_convert_type(packed & jnp.int32(-65536), f32)
    return lo, hi


def _merge_body(oa_ref, on_ref, ga_ref, gb_ref, x_ref, wa_ref, wb_ref, wo_ref, g2_ref, wr_ref, br_ref,
                x1_ref, h2_ref, ri_ref, rf_ref, cnt_ref, carry_ref):
    tm = x_ref.shape[0]

    @pl.when(pl.program_id(0) == 0)
    def _():
        carry_ref[...] = jnp.zeros_like(carry_ref)

    a = jnp.dot(oa_ref[...], wa_ref[...], preferred_element_type=f32)
    b = jnp.dot(on_ref[...], wb_ref[...], preferred_element_type=f32)
    ga = jax.nn.sigmoid(ga_ref[...].astype(f32))
    gb = jax.nn.sigmoid(gb_ref[...].astype(f32))
    merged = (ga * a + gb * b).astype(bf16)
    x1 = x_ref[...] + jnp.dot(merged, wo_ref[...], preferred_element_type=f32)
    x1_ref[...] = x1

    h2 = x1 * lax.rsqrt(jnp.mean(x1 * x1, axis=-1, keepdims=True) + EPS) * g2_ref[...]
    h_hi = h2.astype(bf16)
    h2_ref[...] = _pack_bf16_pairs(h_hi)
    h_lo = (h2 - h_hi.astype(f32)).astype(bf16)
    hw = jnp.dot(h_hi, wr_ref[...], preferred_element_type=f32)
    logits = (hw[:, :LANES] + hw[:, LANES:]
              + jnp.dot(h_lo, wr_ref[:, :LANES], preferred_element_type=f32)) + br_ref[...]

    lane = lax.broadcasted_iota(i32, logits.shape, 1)
    gl = jnp.where(lane < N_GROUPS, logits, NEG_INF)
    gmax = jnp.max(gl, axis=-1, keepdims=True)
    gsel = jnp.min(jnp.where(gl == gmax, lane, LANES), axis=-1, keepdims=True)
    gw = 1.0 / jnp.sum(jnp.exp(gl - gmax), axis=-1, keepdims=True)
    lo = ROUTE_LANE0 + gsel * EXPERTS_PER_GROUP
    el = jnp.where((lane >= lo) & (lane < lo + EXPERTS_PER_GROUP), logits, NEG_INF)
    m1 = jnp.max(el, axis=-1, keepdims=True)
    i1 = jnp.min(jnp.where(el == m1, lane, LANES), axis=-1, keepdims=True)
    el2 = jnp.where(lane == i1, NEG_INF, el)
    m2 = jnp.max(el2, axis=-1, keepdims=True)
    i2 = jnp.min(jnp.where(el2 == m2, lane, LANES), axis=-1, keepdims=True)
    r = jnp.exp(m2 - m1)
    gate1 = gw / (1.0 + r)
    gate2 = gw * r / (1.0 + r)

    oh = ((lane == i1) | (lane == i2)).astype(bf16)
    row = lax.broadcasted_iota(i32, (tm, tm), 0)
    col = lax.broadcasted_iota(i32, (tm, tm), 1)
    tri = (col < row).astype(bf16)
    before = carry_ref[...] + jnp.dot(tri, oh, preferred_element_type=f32)
    r1 = jnp.sum(jnp.where(lane == i1, before, 0.0), axis=-1, keepdims=True).astype(i32)
    r2 = jnp.sum(jnp.where(lane == i2, before, 0.0), axis=-1, keepdims=True).astype(i32)
    carry = carry_ref[...] + jnp.sum(oh.astype(f32), axis=0, keepdims=True)
    carry_ref[...] = carry
    cnt_ref[...] = carry

    ri_ref[...] = jnp.where(lane == 0, i1 - ROUTE_LANE0, jnp.where(lane == 1, i2 - ROUTE_LANE0,
                            jnp.where(lane == 2, r1, jnp.where(lane == 3, r2, 0))))
    rf_ref[...] = jnp.where(lane == 0, gate1, jnp.where(lane == 1, gate2, 0.0))


def _merge(oa, on, proj, x, wa, wb, wo, g2, wr, br, tm=256):
    n, d = x.shape
    tm = min(tm, n)
    aw = oa.shape[1]
    const = lambda shape: pl.BlockSpec(shape, lambda i: (0,) * len(shape), pipeline_mode=pl.Buffered(1))
    return pl.pallas_call(
        _merge_body,
        grid=(n // tm,),
        in_specs=[pl.BlockSpec((tm, aw), lambda i: (i, 0)),
                  pl.BlockSpec((tm, aw), lambda i: (i, 0)),
                  pl.BlockSpec((pl.Element(tm), pl.Element(d)), lambda i: (i * tm, GATE0 * HEAD_DIM)),
                  pl.BlockSpec((pl.Element(tm), pl.Element(d)), lambda i: (i * tm, GATE0 * HEAD_DIM + d)),
                  pl.BlockSpec((tm, d), lambda i: (i, 0)),
                  const((aw, d)), const((aw, d)), const((d, d)), const((1, d)),
                  const((d, 2 * LANES)), const((1, LANES))],
        out_specs=[pl.BlockSpec((tm, d), lambda i: (i, 0)),
                   pl.BlockSpec((tm, d // 2), lambda i: (i, 0)),
                   pl.BlockSpec((tm, LANES), lambda i: (i, 0)),
                   pl.BlockSpec((tm, LANES), lambda i: (i, 0)),
                   pl.BlockSpec((1, LANES), lambda i: (0, 0))],
        out_shape=[jax.ShapeDtypeStruct((n, d), f32), jax.ShapeDtypeStruct((n, d // 2), i32),
                   jax.ShapeDtypeStruct((n, LANES), i32), jax.ShapeDtypeStruct((n, LANES), f32),
                   jax.ShapeDtypeStruct((1, LANES), f32)],
        scratch_shapes=[pltpu.VMEM((1, LANES), f32)],
        compiler_params=_cparams(("arbitrary",)),
        name="merge_route",
    )(oa, on, proj, proj, x, wa, wb, wo, g2, wr, br)


def _moe_body(d1_ref, d2_ref, be_ref, bv_ref, nu_ref, h_hbm, wi_ref, wd_ref, y_hbm,
              xin0, xin1, yo0, yo1, wi_bf, wd_bf, enc_ref, gsem, ssem):
    b = pl.program_id(0)
    nu = nu_ref[0]
    n = d1_ref.shape[0]
    tb = xin0.shape[0]
    nb = be_ref.shape[0]
    xin, yo = (xin0, xin1), (yo0, yo1)

    def gather(blk, r, slot):
        enc = enc_ref[blk * tb + r]
        return pltpu.make_async_copy(h_hbm.at[pl.ds(enc & (n - 1), 1), :], xin[slot].at[pl.ds(r, 1), :],
                                     gsem.at[slot])

    def scatter(blk, r, slot):
        enc = enc_ref[blk * tb + r]
        return pltpu.make_async_copy(yo[slot].at[pl.ds(r, 1), :], y_hbm.at[pl.ds(enc, 1), :], ssem.at[slot])

    def for_rows(blk, fn):
        cnt = bv_ref[blk]
        groups = cnt // DMA_UNROLL

        def unrolled(g, c):
            for u in range(DMA_UNROLL):
                fn(g * DMA_UNROLL + u)
            return c

        def single(r, c):
            fn(r)
            return c

        lax.fori_loop(0, groups, unrolled, 0)
        lax.fori_loop(groups * DMA_UNROLL, cnt, single, 0)

    def for_rows_static(blk, fn):
        cnt = bv_ref[blk]
        for r in range(tb):
            pl.when(r < cnt)(functools.partial(fn, r))

    @pl.when(b == 0)
    def _():
        xin0[...] = jnp.zeros_like(xin0)
        xin1[...] = jnp.zeros_like(xin1)

        def fill(t, c):
            enc_ref[d1_ref[t]] = t
            enc_ref[d2_ref[t]] = n + t
            return c

        lax.fori_loop(0, n, fill, 0, unroll=DMA_UNROLL)
        for_rows(0, lambda r: gather(0, r, 0).start())

    def step(p):
        nxt = jnp.minimum(b + 1, nb - 1)

        @pl.when(b >= 2)
        def _():
            for_rows(b - 2, lambda r: scatter(b - 2, r, p).wait())

        for_rows(b, lambda r: gather(b, r, p).wait())
        for_rows_static(nxt, lambda r: gather(nxt, r, 1 - p).start())
        x_lo, x_hi = _unpack_bf16_pairs(xin[p][...])
        x = jnp.concatenate([x_lo.astype(bf16), x_hi.astype(bf16)], axis=1)
        h = jnp.dot(x, wi_bf[...], preferred_element_type=f32)
        a, u = h[:, :D_EXPERT], h[:, D_EXPERT:]
        act = (a * jax.nn.sigmoid(a) * u).astype(bf16)
        y = jnp.dot(act, wd_bf[...], preferred_element_type=f32)
        yo[p][...] = _pack_bf16_pairs(y.astype(bf16))
        for_rows_static(b, lambda r: scatter(b, r, p).start())

        @pl.when(b == nu - 1)
        def _():
            for_rows(nxt, lambda r: gather(nxt, r, 1 - p).wait())
            for_rows(b, lambda r: scatter(b, r, p).wait())

            @pl.when(b >= 1)
            def _():
                for_rows(b - 1, lambda r: scatter(b - 1, r, 1 - p).wait())

    @pl.when((b < nu) & ((b == 0) | (be_ref[b] != be_ref[jnp.maximum(b - 1, 0)])))
    def _():
        wi_bf[...] = wi_ref[0].astype(bf16)
        wd_bf[...] = wd_ref[0].astype(bf16)

    for p in range(2):
        pl.when((b < nu) & (b % 2 == p))(functools.partial(step, p))


def _moe(d1, d2, blk_expert, blk_valid, n_used, h2, wi, wd):
    n, dp = h2.shape
    d = 2 * dp
    assert n & (n - 1) == 0, "token count must be a power of two (row index = enc & (n - 1))"
    tb = EXPERT_BLOCK
    nb = blk_expert.shape[0]
    return pl.pallas_call(
        _moe_body,
        grid_spec=pltpu.PrefetchScalarGridSpec(
            num_scalar_prefetch=5,
            grid=(nb,),
            in_specs=[pl.BlockSpec(memory_space=pl.ANY),
                      pl.BlockSpec((1, d, 2 * D_EXPERT), lambda b, d1, d2, be, bv, nu: (be[b], 0, 0)),
                      pl.BlockSpec((1, D_EXPERT, d), lambda b, d1, d2, be, bv, nu: (be[b], 0, 0))],
            out_specs=pl.BlockSpec(memory_space=pl.ANY),
            scratch_shapes=[pltpu.VMEM((tb, dp), i32), pltpu.VMEM((tb, dp), i32),
                            pltpu.VMEM((tb, dp), i32), pltpu.VMEM((tb, dp), i32),
                            pltpu.VMEM((d, 2 * D_EXPERT), bf16), pltpu.VMEM((D_EXPERT, d), bf16),
                            pltpu.SMEM((nb * tb,), i32),
                            pltpu.SemaphoreType.DMA((2,)), pltpu.SemaphoreType.DMA((2,))]),
        out_shape=jax.ShapeDtypeStruct((2 * n, dp), i32),
        compiler_params=_cparams(("arbitrary",)),
        name="moe_experts",
    )(d1, d2, blk_expert, blk_valid, n_used, h2, wi, wd)


def _final_body(y1_ref, y2_ref, x1_ref, rf_ref, gf_ref, o_ref):
    rf = rf_ref[...]
    y1 = jnp.concatenate(_unpack_bf16_pairs(y1_ref[...]), axis=1)
    y2 = jnp.concatenate(_unpack_bf16_pairs(y2_ref[...]), axis=1)
    x = x1_ref[...] + (y1 * rf[:, 0:1] + y2 * rf[:, 1:2])
    o_ref[...] = x * lax.rsqrt(jnp.mean(x * x, axis=-1, keepdims=True) + EPS) * gf_ref[...]


def _final(y, x1, rf, gf, tm=512):
    n, d = x1.shape
    tm = min(tm, n)
    return pl.pallas_call(
        _final_body,
        grid=(n // tm,),
        in_specs=[pl.BlockSpec((tm, d // 2), lambda i: (i, 0)),
                  pl.BlockSpec((tm, d // 2), lambda i: (i + n // tm, 0)),
                  pl.BlockSpec((tm, d), lambda i: (i, 0)),
                  pl.BlockSpec((tm, LANES), lambda i: (i, 0)),
                  pl.BlockSpec((1, d), lambda i: (0, 0))],
        out_specs=pl.BlockSpec((tm, d), lambda i: (i, 0)),
        out_shape=jax.ShapeDtypeStruct((n, d), f32),
        compiler_params=_cparams(("parallel",)),
        name="moe_final",
    )(y, y, x1, rf, gf)


def _rope_tables(seq):
    t = jnp.arange(seq, dtype=i32)
    pos = jnp.stack([t // GRID_W, t % GRID_W], axis=-1).astype(f32)
    half = HEAD_DIM // 2
    inv_freq = ROPE_THETA ** (-jnp.arange(0, half, 2, dtype=f32) / half)
    ang = pos[:, :, None] * inv_freq[None, None, :]
    ang = jnp.concatenate([ang, ang], axis=-1).reshape(seq, HEAD_DIM)
    lane = jnp.arange(HEAD_DIM, dtype=i32)
    sign = jnp.where((lane % half) < half // 2, -1.0, 1.0).astype(f32)
    return jnp.cos(ang), jnp.sin(ang) * sign


def _layer(x, p):
    batch, seq, d = x.shape
    n = batch * seq
    x2 = x.reshape(n, d)
    assert seq // GRID_W >= NA_BAND_ROWS and seq // Q_BLOCK >= 5, "sequence too short for the 5 block geometries"
    proj = _inproj(x2, p["norm1_g"], p["w_in"])
    q, k = _qkprep(proj, p["rope_cos"], p["rope_sin"], p["q_norm_g"], p["k_norm_g"], seq)
    oa = _gqa(q, k, proj, batch, seq)
    on = _na(proj, p["na_bias"], batch, seq)
    x1, h2, ri, rf, cnt = _merge(oa, on, proj, x2, p["w_branch_a"], p["w_branch_b"], p["w_out"], p["norm2_g"],
                                 p["w_router"], p["b_router"])

    tb = EXPERT_BLOCK
    n_rows = 2 * n + N_EXPERTS * tb
    counts = cnt[0, ROUTE_LANE0:ROUTE_LANE0 + N_EXPERTS].astype(i32)
    padded = (counts + tb - 1) // tb * tb
    pad_end = jnp.cumsum(padded)
    pad_start = pad_end - padded
    eids = jnp.arange(N_EXPERTS, dtype=i32)

    def start_of(e):
        return jnp.sum(jnp.where(e[:, None] == eids[None, :], pad_start[None, :], 0), axis=1)

    d1 = start_of(ri[:, 0]) + ri[:, 2]
    d2 = start_of(ri[:, 1]) + ri[:, 3]
    nb = n_rows // tb
    blk_start = jnp.arange(nb, dtype=i32) * tb
    blk_expert = jnp.minimum(jnp.sum((pad_end[None, :] <= blk_start[:, None]).astype(i32), axis=1), N_EXPERTS - 1)
    of_blk = blk_expert[:, None] == eids[None, :]
    row_in_expert = blk_start - jnp.sum(jnp.where(of_blk, pad_start[None, :], 0), axis=1)
    blk_valid = jnp.clip(jnp.sum(jnp.where(of_blk, counts[None, :], 0), axis=1) - row_in_expert, 0, tb)
    blk_valid = jnp.where(blk_start < pad_end[-1], blk_valid, 0).astype(i32)
    n_used = (pad_end[-1:] // tb).astype(i32)

    y = _moe(d1, d2, blk_expert, blk_valid, n_used, h2, p["w_exp_in"], p["w_exp_down"])
    out = _final(y, x1, rf, p["norm_f_g"])
    return out.reshape(batch, seq, d)


def kernel(x_prompt, x_sample, norm1_g, w_in, q_norm_g, k_norm_g, na_rpb, w_branch_a, w_branch_b, w_out, norm2_g,
           w_router_group, b_router_group, w_router_expert, b_router_expert, w_exp_in, w_exp_down, norm_f_g):
    assert norm1_g.shape[0] == 1, "one encoder layer"
    w_r = jnp.zeros((D_MODEL, LANES), f32)
    w_r = w_r.at[:, :N_GROUPS].set(w_router_group[0]).at[:, ROUTE_LANE0:ROUTE_LANE0 + N_EXPERTS].set(w_router_expert[0])
    b_r = jnp.zeros((1, LANES), f32)
    b_r = b_r.at[0, :N_GROUPS].set(b_router_group[0]).at[0, ROUTE_LANE0:ROUTE_LANE0 + N_EXPERTS].set(b_router_expert[0])
    w_r_hi = w_r.astype(bf16)
    p = {
        "norm1_g": norm1_g[0][None], "w_in": w_in[0].astype(bf16),
        "q_norm_g": q_norm_g[0][None], "k_norm_g": k_norm_g[0][None],
        "w_branch_a": w_branch_a[0].astype(bf16), "w_branch_b": w_branch_b[0].astype(bf16),
        "w_out": w_out[0].astype(bf16), "norm2_g": norm2_g[0][None],
        "w_router": jnp.concatenate([w_r_hi, (w_r - w_r_hi.astype(f32)).astype(bf16)], axis=1), "b_router": b_r,
        "w_exp_in": w_exp_in[0], "w_exp_down": w_exp_down[0],
        "norm_f_g": norm_f_g[None],
    }
    max_seq = max(x_prompt.shape[1], x_sample.shape[1])
    p["rope_cos"], p["rope_sin"] = _rope_tables(max_seq)
    p["na_bias"] = _na_bias_tables(na_rpb[0], max_seq // GRID_W)
    return _layer(x_prompt, p), _layer(x_sample, p)
```

```python
import functools

import jax
import jax.numpy as jnp
from jax import lax
from jax.experimental import pallas as pl
from jax.experimental.pallas import tpu as pltpu

f32 = jnp.float32
bf16 = jnp.bfloat16
i32 = jnp.int32

D_MODEL = 2048
GRID_W = 64
HEAD_DIM = 128
A_HEADS = 8
A_KV_HEADS = 2
A_GROUP = A_HEADS // A_KV_HEADS
NA_HEADS = 8
NA_WIN_ROWS = 8
NA_WIN_COLS = 16
Q_BLOCK = 128
ROPE_THETA = 10000.0
N_GROUPS = 4
EXPERTS_PER_GROUP = 8
N_EXPERTS = N_GROUPS * EXPERTS_PER_GROUP
D_EXPERT = D_MODEL // 4
EPS = 1e-6
NEG_INF = -1e30
SCALE = HEAD_DIM ** -0.5
LOG2E = 1.4426950408889634

QA0, KA0, VA0, QN0, KN0, VN0, GATE0 = 0, 8, 10, 12, 20, 28, 36
IN_COLS = (GATE0 + 2 * D_MODEL // HEAD_DIM) * HEAD_DIM

LANES = 128
NA_BAND_ROWS = 10
NA_KEYS = NA_BAND_ROWS * GRID_W
NA_PAIR_ROWS = NA_BAND_ROWS + Q_BLOCK // GRID_W
ROUTE_LANE0 = 4
EXPERT_BLOCK = 256
DMA_UNROLL = 8
VMEM_LIMIT = 56 * 1024 * 1024


def _cparams(sem):
    return pltpu.CompilerParams(dimension_semantics=sem, vmem_limit_bytes=VMEM_LIMIT)


def _inproj_body(x_ref, g_ref, w_ref, o_ref, h_ref):
    @pl.when(pl.program_id(1) == 0)
    def _():
        x = x_ref[...]
        ms = jnp.mean(x * x, axis=-1, keepdims=True)
        h_ref[...] = (x * lax.rsqrt(ms + EPS) * g_ref[...]).astype(bf16)

    o_ref[...] = jnp.dot(h_ref[...], w_ref[...], preferred_element_type=f32).astype(o_ref.dtype)


def _inproj(x, g, w, tm=1024, tn=IN_COLS // 4):
    n, d = x.shape
    c = w.shape[1]
    tm = min(tm, n)
    return pl.pallas_call(
        _inproj_body,
        grid=(n // tm, c // tn),
        in_specs=[pl.BlockSpec((tm, d), lambda i, j: (i, 0)),
                  pl.BlockSpec((1, d), lambda i, j: (0, 0)),
                  pl.BlockSpec((d, tn), lambda i, j: (0, j))],
        out_specs=pl.BlockSpec((tm, tn), lambda i, j: (i, j)),
        out_shape=jax.ShapeDtypeStruct((n, c), bf16),
        scratch_shapes=[pltpu.VMEM((tm, d), bf16)],
        compiler_params=_cparams(("parallel", "arbitrary")),
        name="inproj",
    )(x, g, w)


def _qkprep_body(q_ref, k_ref, cos_ref, sin_ref, qg_ref, kg_ref, qo_ref, ko_ref):
    cos = cos_ref[...]
    sin = sin_ref[...]
    lane = lax.broadcasted_iota(i32, cos.shape, 1)
    first = (lane % (HEAD_DIM // 2)) < (HEAD_DIM // 4)

    def prep(x, g, scale):
        x = x.astype(f32)
        y = x * lax.rsqrt(jnp.mean(x * x, axis=-1, keepdims=True) + EPS) * g
        rot = jnp.where(first, pltpu.roll(y, HEAD_DIM - HEAD_DIM // 4, 1), pltpu.roll(y, HEAD_DIM // 4, 1))
        return (y * cos + rot * sin) * scale

    for h in range(A_HEADS):
        sl = slice(h * HEAD_DIM, (h + 1) * HEAD_DIM)
        qo_ref[:, sl] = prep(q_ref[:, sl], qg_ref[...], SCALE * LOG2E).astype(bf16)
    for h in range(A_KV_HEADS):
        sl = slice(h * HEAD_DIM, (h + 1) * HEAD_DIM)
        ko_ref[:, sl] = prep(k_ref[:, sl], kg_ref[...], 1.0).astype(bf16)


def _qkprep(proj, cos, sin, qg, kg, seq, tm=512):
    n = proj.shape[0]
    tm = min(tm, seq)
    spb = seq // tm
    qw, kw = A_HEADS * HEAD_DIM, A_KV_HEADS * HEAD_DIM
    return pl.pallas_call(
        _qkprep_body,
        grid=(n // tm,),
        in_specs=[pl.BlockSpec((tm, qw), lambda i: (i, QA0 * HEAD_DIM // qw)),
                  pl.BlockSpec((tm, kw), lambda i: (i, KA0 * HEAD_DIM // kw)),
                  pl.BlockSpec((tm, HEAD_DIM), lambda i: (i % spb, 0)),
                  pl.BlockSpec((tm, HEAD_DIM), lambda i: (i % spb, 0)),
                  pl.BlockSpec((1, HEAD_DIM), lambda i: (0, 0)),
                  pl.BlockSpec((1, HEAD_DIM), lambda i: (0, 0))],
        out_specs=[pl.BlockSpec((tm, qw), lambda i: (i, 0)),
                   pl.BlockSpec((tm, kw), lambda i: (i, 0))],
        out_shape=[jax.ShapeDtypeStruct((n, qw), bf16), jax.ShapeDtypeStruct((n, kw), bf16)],
        compiler_params=_cparams(("parallel",)),
        name="qkprep",
    )(proj, proj, cos, sin, qg, kg)


def _lane_tile(x, n):
    return jnp.concatenate([x] * n, axis=1)


def _gqa_body(q_ref, k_ref, v_ref, o_ref, m_ref, acc_ref, *, tk):
    nkv = k_ref.shape[0] // tk
    m_ref[...] = jnp.full(m_ref.shape, NEG_INF, f32)
    acc_ref[...] = jnp.zeros(acc_ref.shape, f32)
    ones = jnp.ones((tk, HEAD_DIM), bf16)

    def step(j, c):
        off = pl.multiple_of(j * tk, tk)
        k = k_ref[pl.ds(off, tk), :]
        v1 = jnp.concatenate([v_ref[pl.ds(off, tk), :], ones], axis=1)

        def scores(g):
            q = q_ref[:, g * HEAD_DIM:(g + 1) * HEAD_DIM]
            return lax.dot_general(q, k, (((1,), (1,)), ((), ())), preferred_element_type=f32)

        s = scores(0)
        for g in range(A_GROUP):
            s_next = scores(g + 1) if g + 1 < A_GROUP else None
            m = m_ref[g]
            m_new = jnp.maximum(m, jnp.max(s, axis=-1, keepdims=True))
            alpha = jnp.exp2(m - m_new)
            p = jnp.exp2(s - _lane_tile(m_new, tk // LANES))
            pv = jnp.dot(p.astype(bf16), v1, preferred_element_type=f32)
            acc_ref[g] = _lane_tile(alpha, 2) * acc_ref[g] + pv
            m_ref[g] = m_new
            s = s_next
        return c

    lax.fori_loop(0, nkv, step, 0)
    for g in range(A_GROUP):
        acc = acc_ref[g]
        o_ref[:, g * HEAD_DIM:(g + 1) * HEAD_DIM] = (acc[:, :HEAD_DIM] / acc[:, HEAD_DIM:]).astype(o_ref.dtype)


def _gqa(q, k, proj, batch, seq, tq=1024, tk=2048):
    n = q.shape[0]
    tq, tk = min(tq, seq), min(tk, seq)
    nq = seq // tq
    gw = A_GROUP * HEAD_DIM
    return pl.pallas_call(
        functools.partial(_gqa_body, tk=tk),
        grid=(batch, A_KV_HEADS, nq),
        in_specs=[pl.BlockSpec((tq, gw), lambda b, h, i: (b * nq + i, h)),
                  pl.BlockSpec((seq, HEAD_DIM), lambda b, h, i: (b, h)),
                  pl.BlockSpec((seq, HEAD_DIM), lambda b, h, i: (b, VA0 + h))],
        out_specs=pl.BlockSpec((tq, gw), lambda b, h, i: (b * nq + i, h)),
        out_shape=jax.ShapeDtypeStruct((n, A_HEADS * HEAD_DIM), bf16),
        scratch_shapes=[pltpu.VMEM((A_GROUP, tq, LANES), f32), pltpu.VMEM((A_GROUP, tq, 2 * HEAD_DIM), f32)],
        compiler_params=_cparams(("parallel", "parallel", "arbitrary")),
        name="gqa",
    )(q, k, proj)


def _na_bias_tables(rpb, rows):
    nblk = rows * GRID_W // Q_BLOCK
    rpq = Q_BLOCK // GRID_W
    ncls = 5
    blk = jnp.array([0, 1, 2, nblk - 2, nblk - 1], i32)
    col = jnp.arange(GRID_W, dtype=i32)
    col_start = jnp.clip(col - NA_WIN_COLS // 2, 0, GRID_W - NA_WIN_COLS)
    col_in = (col[None, :] >= col_start[:, None]) & (col[None, :] < col_start[:, None] + NA_WIN_COLS)
    dc = jnp.clip(col[None, :] - col[:, None] + NA_WIN_COLS - 1, 0, 2 * NA_WIN_COLS - 2)
    r0 = blk * rpq
    bs = jnp.clip(r0 - NA_WIN_ROWS // 2, 0, rows - NA_BAND_ROWS)
    q_row = r0[:, None] + jnp.arange(rpq, dtype=i32)[None, :]
    k_row = bs[:, None] + jnp.arange(NA_BAND_ROWS, dtype=i32)[None, :]
    row_start = jnp.clip(q_row - NA_WIN_ROWS // 2, 0, rows - NA_WIN_ROWS)
    row_in = (k_row[:, None, :] >= row_start[:, :, None]) & (k_row[:, None, :] < row_start[:, :, None] + NA_WIN_ROWS)
    dr = jnp.clip(k_row[:, None, :] - q_row[:, :, None] + NA_WIN_ROWS - 1, 0, 2 * NA_WIN_ROWS - 2)
    picked = rpb.astype(f32)[:, dr, :]
    onehot = (dc[None] == jnp.arange(2 * NA_WIN_COLS - 1, dtype=i32)[:, None, None]).astype(f32)
    bias = jnp.einsum("hcqkd,dxy->chqxky", picked, onehot, precision=lax.Precision.HIGHEST)
    mask = row_in[:, None, :, None, :, None] & col_in[None, None, None, :, None, :]
    bias = jnp.where(mask, bias * LOG2E, NEG_INF)
    return bias.reshape(ncls, NA_HEADS, Q_BLOCK, NA_KEYS)


def _na_band_row(first_q_row, rows, band_rows):
    return jnp.clip(first_q_row - NA_WIN_ROWS // 2, 0, rows - band_rows)


def _na_body(q_ref, k_ref, v_ref, tab0_ref, tab1_ref, o_ref, *, rows):
    nh = q_ref.shape[1] // HEAD_DIM
    rpq = Q_BLOCK // GRID_W
    first_q_row = pl.program_id(1) * (2 * rpq)
    win_row = _na_band_row(first_q_row, rows, NA_PAIR_ROWS)
    tabs = (tab0_ref, tab1_ref)
    offs = [pl.multiple_of((_na_band_row(first_q_row + j * rpq, rows, NA_BAND_ROWS) - win_row) * GRID_W, GRID_W)
            for j in range(2)]
    ones = jnp.ones((NA_KEYS, HEAD_DIM), bf16)
    units = [(j, h) for j in range(2) for h in range(nh)]

    def scores(j, h):
        sl = slice(h * HEAD_DIM, (h + 1) * HEAD_DIM)
        q = q_ref[j * Q_BLOCK:(j + 1) * Q_BLOCK, sl]
        k = k_ref[pl.ds(offs[j], NA_KEYS), sl]
        s = lax.dot_general(q, k, (((1,), (1,)), ((), ())), preferred_element_type=f32)
        return s * (SCALE * LOG2E) + tabs[j][0, h]

    s = scores(*units[0])
    for u, (j, h) in enumerate(units):
        sl = slice(h * HEAD_DIM, (h + 1) * HEAD_DIM)
        s_next = scores(*units[u + 1]) if u + 1 < len(units) else None
        m = jnp.broadcast_to(jnp.max(s, axis=-1, keepdims=True), (Q_BLOCK, LANES))
        p = jnp.exp2(s - _lane_tile(m, NA_KEYS // LANES))
        v1 = jnp.concatenate([v_ref[pl.ds(offs[j], NA_KEYS), sl], ones], axis=1)
        pv = jnp.dot(p.astype(bf16), v1, preferred_element_type=f32)
        o_ref[j * Q_BLOCK:(j + 1) * Q_BLOCK, sl] = (pv[:, :HEAD_DIM] / pv[:, HEAD_DIM:]).astype(o_ref.dtype)
        s = s_next


def _na(proj, tab, batch, seq):
    n = proj.shape[0]
    nblk = seq // Q_BLOCK
    rows = seq // GRID_W
    rpq = Q_BLOCK // GRID_W
    w = NA_HEADS * HEAD_DIM
    assert nblk % 2 == 0 and rows >= NA_PAIR_ROWS
    npair = nblk // 2
    win_keys = NA_PAIR_ROWS * GRID_W

    def window(b, i):
        return pl.multiple_of(b * seq + _na_band_row(i * 2 * rpq, rows, NA_PAIR_ROWS) * GRID_W, GRID_W)

    def cls(blk):
        return jnp.where(blk < 2, blk, jnp.where(blk <= nblk - 3, 2, blk - (nblk - 5)))

    return pl.pallas_call(
        functools.partial(_na_body, rows=rows),
        grid=(batch, npair),
        in_specs=[pl.BlockSpec((pl.Element(2 * Q_BLOCK), pl.Element(w)),
                               lambda b, i: ((b * npair + i) * 2 * Q_BLOCK, QN0 * HEAD_DIM)),
                  pl.BlockSpec((pl.Element(win_keys), pl.Element(w)), lambda b, i: (window(b, i), KN0 * HEAD_DIM)),
                  pl.BlockSpec((pl.Element(win_keys), pl.Element(w)), lambda b, i: (window(b, i), VN0 * HEAD_DIM)),
                  pl.BlockSpec((1, NA_HEADS, Q_BLOCK, NA_KEYS), lambda b, i: (cls(2 * i), 0, 0, 0)),
                  pl.BlockSpec((1, NA_HEADS, Q_BLOCK, NA_KEYS), lambda b, i: (cls(2 * i + 1), 0, 0, 0))],
        out_specs=pl.BlockSpec((2 * Q_BLOCK, w), lambda b, i: (b * npair + i, 0)),
        out_shape=jax.ShapeDtypeStruct((n, w), bf16),
        compiler_params=_cparams(("parallel", "arbitrary")),
        name="natten",
    )(proj, proj, proj, tab, tab)


def _pack_bf16_pairs(x):
    c = x.shape[1] // 2
    bits = lax.bitcast_convert_type(x.astype(f32), i32)
    return lax.shift_right_logical(bits[:, :c], 16) | bits[:, c:]


def _unpack_bf16_pairs(packed):
    lo = lax.bitcast_convert_type(lax.shift_left(packed, 16), f32)
    hi = lax.bitcast_convert_type(packed & jnp.int32(-65536), f32)
    return lo, hi


def _merge_body(oa_ref, on_ref, ga_ref, gb_ref, x_ref, wa_ref, wb_ref, wo_ref, g2_ref, wr_ref, br_ref,
                x1_ref, h2_ref, ri_ref, rf_ref, cnt_ref, carry_ref):
    tm = x_ref.shape[0]

    @pl.when(pl.program_id(0) == 0)
    def _():
        carry_ref[...] = jnp.zeros_like(carry_ref)

    a = jnp.dot(oa_ref[...], wa_ref[...], preferred_element_type=f32)
    b = jnp.dot(on_ref[...], wb_ref[...], preferred_element_type=f32)
    ga = jax.nn.sigmoid(ga_ref[...].astype(f32))
    gb = jax.nn.sigmoid(gb_ref[...].astype(f32))
    merged = (ga * a + gb * b).astype(bf16)
    x1 = x_ref[...] + jnp.dot(merged, wo_ref[...], preferred_element_type=f32)
    x1_ref[...] = x1

    h2 = x1 * lax.rsqrt(jnp.mean(x1 * x1, axis=-1, keepdims=True) + EPS) * g2_ref[...]
    h_hi = h2.astype(bf16)
    h2_ref[...] = _pack_bf16_pairs(h_hi)
    h_lo = (h2 - h_hi.astype(f32)).astype(bf16)
    hw = jnp.dot(h_hi, wr_ref[...], preferred_element_type=f32)
    logits = (hw[:, :LANES] + hw[:, LANES:]
              + jnp.dot(h_lo, wr_ref[:, :LANES], preferred_element_type=f32)) + br_ref[...]

    lane = lax.broadcasted_iota(i32, logits.shape, 1)
    gl = jnp.where(lane < N_GROUPS, logits, NEG_INF)
    gmax = jnp.max(gl, axis=-1, keepdims=True)
    gsel = jnp.min(jnp.where(gl == gmax, lane, LANES), axis=-1, keepdims=True)
    gw = 1.0 / jnp.sum(jnp.exp(gl - gmax), axis=-1, keepdims=True)
    lo = ROUTE_LANE0 + gsel * EXPERTS_PER_GROUP
    el = jnp.where((lane >= lo) & (lane < lo + EXPERTS_PER_GROUP), logits, NEG_INF)
    m1 = jnp.max(el, axis=-1, keepdims=True)
    i1 = jnp.min(jnp.where(el == m1, lane, LANES), axis=-1, keepdims=True)
    el2 = jnp.where(lane == i1, NEG_INF, el)
    m2 = jnp.max(el2, axis=-1, keepdims=True)
    i2 = jnp.min(jnp.where(el2 == m2, lane, LANES), axis=-1, keepdims=True)
    r = jnp.exp(m2 - m1)
    gate1 = gw / (1.0 + r)
    gate2 = gw * r / (1.0 + r)

    oh = ((lane == i1) | (lane == i2)).astype(bf16)
    row = lax.broadcasted_iota(i32, (tm, tm), 0)
    col = lax.broadcasted_iota(i32, (tm, tm), 1)
    tri = (col < row).astype(bf16)
    before = carry_ref[...] + jnp.dot(tri, oh, preferred_element_type=f32)
    r1 = jnp.sum(jnp.where(lane == i1, before, 0.0), axis=-1, keepdims=True).astype(i32)
    r2 = jnp.sum(jnp.where(lane == i2, before, 0.0), axis=-1, keepdims=True).astype(i32)
    carry = carry_ref[...] + jnp.sum(oh.astype(f32), axis=0, keepdims=True)
    carry_ref[...] = carry
    cnt_ref[...] = carry

    ri_ref[...] = jnp.where(lane == 0, i1 - ROUTE_LANE0, jnp.where(lane == 1, i2 - ROUTE_LANE0,
                            jnp.where(lane == 2, r1, jnp.where(lane == 3, r2, 0))))
    rf_ref[...] = jnp.where(lane == 0, gate1, jnp.where(lane == 1, gate2, 0.0))


def _merge(oa, on, proj, x, wa, wb, wo, g2, wr, br, tm=256):
    n, d = x.shape
    tm = min(tm, n)
    aw = oa.shape[1]
    const = lambda shape: pl.BlockSpec(shape, lambda i: (0,) * len(shape), pipeline_mode=pl.Buffered(1))
    return pl.pallas_call(
        _merge_body,
        grid=(n // tm,),
        in_specs=[pl.BlockSpec((tm, aw), lambda i: (i, 0)),
                  pl.BlockSpec((tm, aw), lambda i: (i, 0)),
                  pl.BlockSpec((pl.Element(tm), pl.Element(d)), lambda i: (i * tm, GATE0 * HEAD_DIM)),
                  pl.BlockSpec((pl.Element(tm), pl.Element(d)), lambda i: (i * tm, GATE0 * HEAD_DIM + d)),
                  pl.BlockSpec((tm, d), lambda i: (i, 0)),
                  const((aw, d)), const((aw, d)), const((d, d)), const((1, d)),
                  const((d, 2 * LANES)), const((1, LANES))],
        out_specs=[pl.BlockSpec((tm, d), lambda i: (i, 0)),
                   pl.BlockSpec((tm, d // 2), lambda i: (i, 0)),
                   pl.BlockSpec((tm, LANES), lambda i: (i, 0)),
                   pl.BlockSpec((tm, LANES), lambda i: (i, 0)),
                   pl.BlockSpec((1, LANES), lambda i: (0, 0))],
        out_shape=[jax.ShapeDtypeStruct((n, d), f32), jax.ShapeDtypeStruct((n, d // 2), i32),
                   jax.ShapeDtypeStruct((n, LANES), i32), jax.ShapeDtypeStruct((n, LANES), f32),
                   jax.ShapeDtypeStruct((1, LANES), f32)],
        scratch_shapes=[pltpu.VMEM((1, LANES), f32)],
        compiler_params=_cparams(("arbitrary",)),
        name="merge_route",
    )(oa, on, proj, proj, x, wa, wb, wo, g2, wr, br)


def _moe_body(d1_ref, d2_ref, be_ref, bv_ref, nu_ref, h_hbm, wi_ref, wd_ref, y_hbm,
              xin0, xin1, yo0, yo1, wi_bf, wd_bf, enc_ref, gsem, ssem):
    b = pl.program_id(0)
    nu = nu_ref[0]
    n = d1_ref.shape[0]
    tb = xin0.shape[0]
    nb = be_ref.shape[0]
    xin, yo = (xin0, xin1), (yo0, yo1)

    def gather(blk, r, slot):
        enc = enc_ref[blk * tb + r]
        return pltpu.make_async_copy(h_hbm.at[pl.ds(enc & (n - 1), 1), :], xin[slot].at[pl.ds(r, 1), :],
                                     gsem.at[slot])

    def scatter(blk, r, slot):
        enc = enc_ref[blk * tb + r]
        return pltpu.make_async_copy(yo[slot].at[pl.ds(r, 1), :], y_hbm.at[pl.ds(enc, 1), :], ssem.at[slot])

    def for_rows(blk, fn):
        cnt = bv_ref[blk]
        groups = cnt // DMA_UNROLL

        def unrolled(g, c):
            for u in range(DMA_UNROLL):
                fn(g * DMA_UNROLL + u)
            return c

        def single(r, c):
            fn(r)
            return c

        lax.fori_loop(0, groups, unrolled, 0)
        lax.fori_loop(groups * DMA_UNROLL, cnt, single, 0)

    def for_rows_static(blk, fn):
        cnt = bv_ref[blk]
        for r in range(tb):
            pl.when(r < cnt)(functools.partial(fn, r))

    @pl.when(b == 0)
    def _():
        xin0[...] = jnp.zeros_like(xin0)
        xin1[...] = jnp.zeros_like(xin1)

        def fill(t, c):
            enc_ref[d1_ref[t]] = t
            enc_ref[d2_ref[t]] = n + t
            return c

        lax.fori_loop(0, n, fill, 0, unroll=DMA_UNROLL)
        for_rows(0, lambda r: gather(0, r, 0).start())

    def step(p):
        nxt = jnp.minimum(b + 1, nb - 1)

        @pl.when(b >= 2)
        def _():
            for_rows(b - 2, lambda r: scatter(b - 2, r, p).wait())

        for_rows(b, lambda r: gather(b, r, p).wait())
        for_rows_static(nxt, lambda r: gather(nxt, r, 1 - p).start(priority=r % 2))
        x_lo, x_hi = _unpack_bf16_pairs(xin[p][...])
        x = jnp.concatenate([x_lo.astype(bf16), x_hi.astype(bf16)], axis=1)
        h = jnp.dot(x, wi_bf[...], preferred_element_type=f32)
        a, u = h[:, :D_EXPERT], h[:, D_EXPERT:]
        act = (a * jax.nn.sigmoid(a) * u).astype(bf16)
        y = jnp.dot(act, wd_bf[...], preferred_element_type=f32)
        yo[p][...] = _pack_bf16_pairs(y.astype(bf16))
        for_rows_static(b, lambda r: scatter(b, r, p).start(priority=r % 2))

        @pl.when(b == nu - 1)
        def _():
            for_rows(nxt, lambda r: gather(nxt, r, 1 - p).wait())
            for_rows(b, lambda r: scatter(b, r, p).wait())

            @pl.when(b >= 1)
            def _():
                for_rows(b - 1, lambda r: scatter(b - 1, r, 1 - p).wait())

    @pl.when((b < nu) & ((b == 0) | (be_ref[b] != be_ref[jnp.maximum(b - 1, 0)])))
    def _():
        wi_bf[...] = wi_ref[0].astype(bf16)
        wd_bf[...] = wd_ref[0].astype(bf16)

    for p in range(2):
        pl.when((b < nu) & (b % 2 == p))(functools.partial(step, p))


def _moe(d1, d2, blk_expert, blk_valid, n_used, h2, wi, wd):
    n, dp = h2.shape
    d = 2 * dp
    assert n & (n - 1) == 0, "token count must be a power of two (row index = enc & (n - 1))"
    tb = EXPERT_BLOCK
    nb = blk_expert.shape[0]
    return pl.pallas_call(
        _moe_body,
        grid_spec=pltpu.PrefetchScalarGridSpec(
            num_scalar_prefetch=5,
            grid=(nb,),
            in_specs=[pl.BlockSpec(memory_space=pl.ANY),
                      pl.BlockSpec((1, d, 2 * D_EXPERT), lambda b, d1, d2, be, bv, nu: (be[b], 0, 0)),
                      pl.BlockSpec((1, D_EXPERT, d), lambda b, d1, d2, be, bv, nu: (be[b], 0, 0))],
            out_specs=pl.BlockSpec(memory_space=pl.ANY),
            scratch_shapes=[pltpu.VMEM((tb, dp), i32), pltpu.VMEM((tb, dp), i32),
                            pltpu.VMEM((tb, dp), i32), pltpu.VMEM((tb, dp), i32),
                            pltpu.VMEM((d, 2 * D_EXPERT), bf16), pltpu.VMEM((D_EXPERT, d), bf16),
                            pltpu.SMEM((nb * tb,), i32),
                            pltpu.SemaphoreType.DMA((2,)), pltpu.SemaphoreType.DMA((2,))]),
        out_shape=jax.ShapeDtypeStruct((2 * n, dp), i32),
        compiler_params=_cparams(("arbitrary",)),
        name="moe_experts",
    )(d1, d2, blk_expert, blk_valid, n_used, h2, wi, wd)


def _final_body(y1_ref, y2_ref, x1_ref, rf_ref, gf_ref, o_ref):
    rf = rf_ref[...]
    y1 = jnp.concatenate(_unpack_bf16_pairs(y1_ref[...]), axis=1)
    y2 = jnp.concatenate(_unpack_bf16_pairs(y2_ref[...]), axis=1)
    x = x1_ref[...] + (y1 * rf[:, 0:1] + y2 * rf[:, 1:2])
    o_ref[...] = x * lax.rsqrt(jnp.mean(x * x, axis=-1, keepdims=True) + EPS) * gf_ref[...]


def _final(y, x1, rf, gf, tm=512):
    n, d = x1.shape
    tm = min(tm, n)
    return pl.pallas_call(
        _final_body,
        grid=(n // tm,),
        in_specs=[pl.BlockSpec((tm, d // 2), lambda i: (i, 0)),
                  pl.BlockSpec((tm, d // 2), lambda i: (i + n // tm, 0)),
                  pl.BlockSpec((tm, d), lambda i: (i, 0)),
                  pl.BlockSpec((tm, LANES), lambda i: (i, 0)),
                  pl.BlockSpec((1, d), lambda i: (0, 0))],
        out_specs=pl.BlockSpec((tm, d), lambda i: (i, 0)),
        out_shape=jax.ShapeDtypeStruct((n, d), f32),
        compiler_params=_cparams(("parallel",)),
        name="moe_final",
    )(y, y, x1, rf, gf)


def _rope_tables(seq):
    t = jnp.arange(seq, dtype=i32)
    pos = jnp.stack([t // GRID_W, t % GRID_W], axis=-1).astype(f32)
    half = HEAD_DIM // 2
    inv_freq = ROPE_THETA ** (-jnp.arange(0, half, 2, dtype=f32) / half)
    ang = pos[:, :, None] * inv_freq[None, None, :]
    ang = jnp.concatenate([ang, ang], axis=-1).reshape(seq, HEAD_DIM)
    lane = jnp.arange(HEAD_DIM, dtype=i32)
    sign = jnp.where((lane % half) < half // 2, -1.0, 1.0).astype(f32)
    return jnp.cos(ang), jnp.sin(ang) * sign


def _layer(x, p):
    batch, seq, d = x.shape
    n = batch * seq
    x2 = x.reshape(n, d)
    assert seq // GRID_W >= NA_BAND_ROWS and seq // Q_BLOCK >= 5, "sequence too short for the 5 block geometries"
    proj = _inproj(x2, p["norm1_g"], p["w_in"])
    q, k = _qkprep(proj, p["rope_cos"], p["rope_sin"], p["q_norm_g"], p["k_norm_g"], seq)
    oa = _gqa(q, k, proj, batch, seq)
    on = _na(proj, p["na_bias"], batch, seq)
    x1, h2, ri, rf, cnt = _merge(oa, on, proj, x2, p["w_branch_a"], p["w_branch_b"], p["w_out"], p["norm2_g"],
                                 p["w_router"], p["b_router"])

    tb = EXPERT_BLOCK
    n_rows = 2 * n + N_EXPERTS * tb
    counts = cnt[0, ROUTE_LANE0:ROUTE_LANE0 + N_EXPERTS].astype(i32)
    padded = (counts + tb - 1) // tb * tb
    pad_end = jnp.cumsum(padded)
    pad_start = pad_end - padded
    eids = jnp.arange(N_EXPERTS, dtype=i32)

    def start_of(e):
        return jnp.sum(jnp.where(e[:, None] == eids[None, :], pad_start[None, :], 0), axis=1)

    d1 = start_of(ri[:, 0]) + ri[:, 2]
    d2 = start_of(ri[:, 1]) + ri[:, 3]
    nb = n_rows // tb
    blk_start = jnp.arange(nb, dtype=i32) * tb
    blk_expert = jnp.minimum(jnp.sum((pad_end[None, :] <= blk_start[:, None]).astype(i32), axis=1), N_EXPERTS - 1)
    of_blk = blk_expert[:, None] == eids[None, :]
    row_in_expert = blk_start - jnp.sum(jnp.where(of_blk, pad_start[None, :], 0), axis=1)
    blk_valid = jnp.clip(jnp.sum(jnp.where(of_blk, counts[None, :], 0), axis=1) - row_in_expert, 0, tb)
    blk_valid = jnp.where(blk_start < pad_end[-1], blk_valid, 0).astype(i32)
    n_used = (pad_end[-1:] // tb).astype(i32)

    y = _moe(d1, d2, blk_expert, blk_valid, n_used, h2, p["w_exp_in"], p["w_exp_down"])
    out = _final(y, x1, rf, p["norm_f_g"])
    return out.reshape(batch, seq, d)


def kernel(x_prompt, x_sample, norm1_g, w_in, q_norm_g, k_norm_g, na_rpb, w_branch_a, w_branch_b, w_out, norm2_g,
           w_router_group, b_router_group, w_router_expert, b_router_expert, w_exp_in, w_exp_down, norm_f_g):
    assert norm1_g.shape[0] == 1, "one encoder layer"
    w_r = jnp.zeros((D_MODEL, LANES), f32)
    w_r = w_r.at[:, :N_GROUPS].set(w_router_group[0]).at[:, ROUTE_LANE0:ROUTE_LANE0 + N_EXPERTS].set(w_router_expert[0])
    b_r = jnp.zeros((1, LANES), f32)
    b_r = b_r.at[0, :N_GROUPS].set(b_router_group[0]).at[0, ROUTE_LANE0:ROUTE_LANE0 + N_EXPERTS].set(b_router_expert[0])
    w_r_hi = w_r.astype(bf16)
    p = {
        "norm1_g": norm1_g[0][None], "w_in": w_in[0].astype(bf16),
        "q_norm_g": q_norm_g[0][None], "k_norm_g": k_norm_g[0][None],
        "w_branch_a": w_branch_a[0].astype(bf16), "w_branch_b": w_branch_b[0].astype(bf16),
        "w_out": w_out[0].astype(bf16), "norm2_g": norm2_g[0][None],
        "w_router": jnp.concatenate([w_r_hi, (w_r - w_r_hi.astype(f32)).astype(bf16)], axis=1), "b_router": b_r,
        "w_exp_in": w_exp_in[0], "w_exp_down": w_exp_down[0],
        "norm_f_g": norm_f_g[None],
    }
    max_seq = max(x_prompt.shape[1], x_sample.shape[1])
    p["rope_cos"], p["rope_sin"] = _rope_tables(max_seq)
    p["na_bias"] = _na_bias_tables(na_rpb[0], max_seq // GRID_W)
    return _layer(x_prompt, p), _layer(x_sample, p)
```

```python
import functools

import jax
import jax.numpy as jnp
from jax import lax
from jax.experimental import pallas as pl
from jax.experimental.pallas import tpu as pltpu

f32 = jnp.float32
bf16 = jnp.bfloat16
i32 = jnp.int32

D_MODEL = 2048
GRID_W = 64
HEAD_DIM = 128
A_HEADS = 8
A_KV_HEADS = 2
A_GROUP = A_HEADS // A_KV_HEADS
NA_HEADS = 8
NA_WIN_ROWS = 8
NA_WIN_COLS = 16
Q_BLOCK = 128
ROPE_THETA = 10000.0
N_GROUPS = 4
EXPERTS_PER_GROUP = 8
N_EXPERTS = N_GROUPS * EXPERTS_PER_GROUP
D_EXPERT = D_MODEL // 4
EPS = 1e-6
NEG_INF = -1e30
SCALE = HEAD_DIM ** -0.5
LOG2E = 1.4426950408889634

QA0, KA0, VA0, QN0, KN0, VN0, GATE0 = 0, 8, 10, 12, 20, 28, 36
IN_COLS = (GATE0 + 2 * D_MODEL // HEAD_DIM) * HEAD_DIM

LANES = 128
NA_BAND_ROWS = 10
NA_KEYS = NA_BAND_ROWS * GRID_W
NA_PAIR_ROWS = NA_BAND_ROWS + Q_BLOCK // GRID_W
ROUTE_LANE0 = 4
EXPERT_BLOCK = 256
DMA_UNROLL = 8
VMEM_LIMIT = 56 * 1024 * 1024


def _cparams(sem):
    return pltpu.CompilerParams(dimension_semantics=sem, vmem_limit_bytes=VMEM_LIMIT)


def _inproj_body(x_ref, g_ref, w_ref, cos_ref, sin_ref, qg_ref, kg_ref, o_ref, h_ref):
    j = pl.program_id(1)

    @pl.when(j == 0)
    def _():
        x = x_ref[...]
        ms = jnp.mean(x * x, axis=-1, keepdims=True)
        h_ref[...] = (x * lax.rsqrt(ms + EPS) * g_ref[...]).astype(bf16)

    @pl.when(j != 0)
    def _():
        o_ref[...] = jnp.dot(h_ref[...], w_ref[...], preferred_element_type=f32).astype(o_ref.dtype)

    @pl.when(j == 0)
    def _():
        cos = cos_ref[...]
        sin = sin_ref[...]
        lane = lax.broadcasted_iota(i32, cos.shape, 1)
        first = (lane % (HEAD_DIM // 2)) < (HEAD_DIM // 4)

        def prep(x, g, scale):
            y = x * lax.rsqrt(jnp.mean(x * x, axis=-1, keepdims=True) + EPS) * g
            rot = jnp.where(first, pltpu.roll(y, HEAD_DIM - HEAD_DIM // 4, 1), pltpu.roll(y, HEAD_DIM // 4, 1))
            return (y * cos + rot * sin) * scale

        h_in = h_ref[...]
        for h0 in range(0, A_HEADS + A_KV_HEADS, 2):
            acc = jnp.dot(h_in, w_ref[:, h0 * HEAD_DIM:(h0 + 2) * HEAD_DIM], preferred_element_type=f32)
            for h in (h0, h0 + 1):
                is_q = h < A_HEADS
                seg = acc[:, (h - h0) * HEAD_DIM:(h - h0 + 1) * HEAD_DIM]
                o_ref[:, h * HEAD_DIM:(h + 1) * HEAD_DIM] = prep(
                    seg, (qg_ref if is_q else kg_ref)[...], SCALE * LOG2E if is_q else 1.0).astype(o_ref.dtype)
        rest = (A_HEADS + A_KV_HEADS) * HEAD_DIM
        o_ref[:, rest:] = jnp.dot(h_in, w_ref[:, rest:], preferred_element_type=f32).astype(o_ref.dtype)


def _inproj(x, g, w, cos, sin, qg, kg, seq, tm=1024, tn=IN_COLS // 4):
    n, d = x.shape
    c = w.shape[1]
    tm = min(tm, seq)
    spb = seq // tm
    assert QA0 == 0 and KA0 == A_HEADS and tn >= (A_HEADS + A_KV_HEADS) * HEAD_DIM
    return pl.pallas_call(
        _inproj_body,
        grid=(n // tm, c // tn),
        in_specs=[pl.BlockSpec((tm, d), lambda i, j: (i, 0)),
                  pl.BlockSpec((1, d), lambda i, j: (0, 0)),
                  pl.BlockSpec((d, tn), lambda i, j: (0, j)),
                  pl.BlockSpec((tm, HEAD_DIM), lambda i, j: (i % spb, 0)),
                  pl.BlockSpec((tm, HEAD_DIM), lambda i, j: (i % spb, 0)),
                  pl.BlockSpec((1, HEAD_DIM), lambda i, j: (0, 0)),
                  pl.BlockSpec((1, HEAD_DIM), lambda i, j: (0, 0))],
        out_specs=pl.BlockSpec((tm, tn), lambda i, j: (i, j)),
        out_shape=jax.ShapeDtypeStruct((n, c), bf16),
        scratch_shapes=[pltpu.VMEM((tm, d), bf16)],
        compiler_params=_cparams(("parallel", "arbitrary")),
        name="inproj",
    )(x, g, w, cos, sin, qg, kg)


def _lane_tile(x, n):
    return jnp.concatenate([x] * n, axis=1)


def _gqa_body(q_ref, k_ref, v_ref, o_ref, m_ref, acc_ref, *, tk):
    nkv = k_ref.shape[0] // tk
    m_ref[...] = jnp.full(m_ref.shape, NEG_INF, f32)
    acc_ref[...] = jnp.zeros(acc_ref.shape, f32)
    ones = jnp.ones((tk, HEAD_DIM), bf16)

    def step(j, c):
        off = pl.multiple_of(j * tk, tk)
        k = k_ref[pl.ds(off, tk), :]
        v1 = jnp.concatenate([v_ref[pl.ds(off, tk), :], ones], axis=1)

        def scores(g):
            q = q_ref[:, g * HEAD_DIM:(g + 1) * HEAD_DIM]
            return lax.dot_general(q, k, (((1,), (1,)), ((), ())), preferred_element_type=f32)

        s = scores(0)
        for g in range(A_GROUP):
            s_next = scores(g + 1) if g + 1 < A_GROUP else None
            m = m_ref[g]
            m_new = jnp.maximum(m, jnp.max(s, axis=-1, keepdims=True))
            alpha = jnp.exp2(m - m_new)
            p = jnp.exp2(s - _lane_tile(m_new, tk // LANES))
            pv = jnp.dot(p.astype(bf16), v1, preferred_element_type=f32)
            acc_ref[g] = _lane_tile(alpha, 2) * acc_ref[g] + pv
            m_ref[g] = m_new
            s = s_next
        return c

    lax.fori_loop(0, nkv, step, 0)
    for g in range(A_GROUP):
        acc = acc_ref[g]
        o_ref[:, g * HEAD_DIM:(g + 1) * HEAD_DIM] = (acc[:, :HEAD_DIM] / acc[:, HEAD_DIM:]).astype(o_ref.dtype)


def _gqa(proj, batch, seq, tq=1024, tk=2048):
    n = proj.shape[0]
    tq, tk = min(tq, seq), min(tk, seq)
    nq = seq // tq
    gw = A_GROUP * HEAD_DIM
    return pl.pallas_call(
        functools.partial(_gqa_body, tk=tk),
        grid=(batch, A_KV_HEADS, nq),
        in_specs=[pl.BlockSpec((tq, gw), lambda b, h, i: (b * nq + i, QA0 // A_GROUP + h)),
                  pl.BlockSpec((seq, HEAD_DIM), lambda b, h, i: (b, KA0 + h)),
                  pl.BlockSpec((seq, HEAD_DIM), lambda b, h, i: (b, VA0 + h))],
        out_specs=pl.BlockSpec((tq, gw), lambda b, h, i: (b * nq + i, h)),
        out_shape=jax.ShapeDtypeStruct((n, A_HEADS * HEAD_DIM), bf16),
        scratch_shapes=[pltpu.VMEM((A_GROUP, tq, LANES), f32), pltpu.VMEM((A_GROUP, tq, 2 * HEAD_DIM), f32)],
        compiler_params=_cparams(("parallel", "parallel", "arbitrary")),
        name="gqa",
    )(proj, proj, proj)


def _na_bias_tables(rpb, rows):
    nblk = rows * GRID_W // Q_BLOCK
    rpq = Q_BLOCK // GRID_W
    ncls = 5
    blk = jnp.array([0, 1, 2, nblk - 2, nblk - 1], i32)
    col = jnp.arange(GRID_W, dtype=i32)
    col_start = jnp.clip(col - NA_WIN_COLS // 2, 0, GRID_W - NA_WIN_COLS)
    col_in = (col[None, :] >= col_start[:, None]) & (col[None, :] < col_start[:, None] + NA_WIN_COLS)
    dc = jnp.clip(col[None, :] - col[:, None] + NA_WIN_COLS - 1, 0, 2 * NA_WIN_COLS - 2)
    r0 = blk * rpq
    bs = jnp.clip(r0 - NA_WIN_ROWS // 2, 0, rows - NA_BAND_ROWS)
    q_row = r0[:, None] + jnp.arange(rpq, dtype=i32)[None, :]
    k_row = bs[:, None] + jnp.arange(NA_BAND_ROWS, dtype=i32)[None, :]
    row_start = jnp.clip(q_row - NA_WIN_ROWS // 2, 0, rows - NA_WIN_ROWS)
    row_in = (k_row[:, None, :] >= row_start[:, :, None]) & (k_row[:, None, :] < row_start[:, :, None] + NA_WIN_ROWS)
    dr = jnp.clip(k_row[:, None, :] - q_row[:, :, None] + NA_WIN_ROWS - 1, 0, 2 * NA_WIN_ROWS - 2)
    picked = rpb.astype(f32)[:, dr, :]
    onehot = (dc[None] == jnp.arange(2 * NA_WIN_COLS - 1, dtype=i32)[:, None, None]).astype(f32)
    bias = jnp.einsum("hcqkd,dxy->chqxky", picked, onehot, precision=lax.Precision.HIGHEST)
    mask = row_in[:, None, :, None, :, None] & col_in[None, None, None, :, None, :]
    bias = jnp.where(mask, bias * LOG2E, NEG_INF)
    return bias.reshape(ncls, NA_HEADS, Q_BLOCK, NA_KEYS)


def _na_band_row(first_q_row, rows, band_rows):
    return jnp.clip(first_q_row - NA_WIN_ROWS // 2, 0, rows - band_rows)


def _na_body(q_ref, k_ref, v_ref, tab0_ref, tab1_ref, o_ref, *, rows):
    nh = q_ref.shape[1] // HEAD_DIM
    rpq = Q_BLOCK // GRID_W
    first_q_row = pl.program_id(1) * (2 * rpq)
    win_row = _na_band_row(first_q_row, rows, NA_PAIR_ROWS)
    tabs = (tab0_ref, tab1_ref)
    offs = [pl.multiple_of((_na_band_row(first_q_row + j * rpq, rows, NA_BAND_ROWS) - win_row) * GRID_W, GRID_W)
            for j in range(2)]
    ones = jnp.ones((NA_KEYS, HEAD_DIM), bf16)
    units = [(j, h) for j in range(2) for h in range(nh)]

    def scores(j, h):
        sl = slice(h * HEAD_DIM, (h + 1) * HEAD_DIM)
        q = q_ref[j * Q_BLOCK:(j + 1) * Q_BLOCK, sl]
        k = k_ref[pl.ds(offs[j], NA_KEYS), sl]
        s = lax.dot_general(q, k, (((1,), (1,)), ((), ())), preferred_element_type=f32)
        return s * (SCALE * LOG2E) + tabs[j][0, h]

    s = scores(*units[0])
    for u, (j, h) in enumerate(units):
        sl = slice(h * HEAD_DIM, (h + 1) * HEAD_DIM)
        s_next = scores(*units[u + 1]) if u + 1 < len(units) else None
        m = jnp.broadcast_to(jnp.max(s, axis=-1, keepdims=True), (Q_BLOCK, LANES))
        p = jnp.exp2(s - _lane_tile(m, NA_KEYS // LANES))
        v1 = jnp.concatenate([v_ref[pl.ds(offs[j], NA_KEYS), sl], ones], axis=1)
        pv = jnp.dot(p.astype(bf16), v1, preferred_element_type=f32)
        o_ref[j * Q_BLOCK:(j + 1) * Q_BLOCK, sl] = (pv[:, :HEAD_DIM] / pv[:, HEAD_DIM:]).astype(o_ref.dtype)
        s = s_next


def _na(proj, tab, batch, seq):
    n = proj.shape[0]
    nblk = seq // Q_BLOCK
    rows = seq // GRID_W
    rpq = Q_BLOCK // GRID_W
    w = NA_HEADS * HEAD_DIM
    assert nblk % 2 == 0 and rows >= NA_PAIR_ROWS
    npair = nblk // 2
    win_keys = NA_PAIR_ROWS * GRID_W

    def window(b, i):
        return pl.multiple_of(b * seq + _na_band_row(i * 2 * rpq, rows, NA_PAIR_ROWS) * GRID_W, GRID_W)

    def cls(blk):
        return jnp.where(blk < 2, blk, jnp.where(blk <= nblk - 3, 2, blk - (nblk - 5)))

    return pl.pallas_call(
        functools.partial(_na_body, rows=rows),
        grid=(batch, npair),
        in_specs=[pl.BlockSpec((pl.Element(2 * Q_BLOCK), pl.Element(w)),
                               lambda b, i: ((b * npair + i) * 2 * Q_BLOCK, QN0 * HEAD_DIM)),
                  pl.BlockSpec((pl.Element(win_keys), pl.Element(w)), lambda b, i: (window(b, i), KN0 * HEAD_DIM)),
                  pl.BlockSpec((pl.Element(win_keys), pl.Element(w)), lambda b, i: (window(b, i), VN0 * HEAD_DIM)),
                  pl.BlockSpec((1, NA_HEADS, Q_BLOCK, NA_KEYS), lambda b, i: (cls(2 * i), 0, 0, 0)),
                  pl.BlockSpec((1, NA_HEADS, Q_BLOCK, NA_KEYS), lambda b, i: (cls(2 * i + 1), 0, 0, 0))],
        out_specs=pl.BlockSpec((2 * Q_BLOCK, w), lambda b, i: (b * npair + i, 0)),
        out_shape=jax.ShapeDtypeStruct((n, w), bf16),
        compiler_params=_cparams(("parallel", "arbitrary")),
        name="natten",
    )(proj, proj, proj, tab, tab)


def _pack_bf16_pairs(x):
    c = x.shape[1] // 2
    bits = lax.bitcast_convert_type(x.astype(f32), i32)
    return lax.shift_right_logical(bits[:, :c], 16) | bits[:, c:]


def _unpack_bf16_pairs(packed):
    lo = lax.bitcast_convert_type(lax.shift_left(packed, 16), f32)
    hi = lax.bitcast_convert_type(packed & jnp.int32(-65536), f32)
    return lo, hi


def _merge_body(oa_ref, on_ref, ga_ref, gb_ref, x_ref, wa_ref, wb_ref, wo_ref, g2_ref, wr_ref, br_ref,
                x1_ref, h2_ref, ri_ref, rf_ref, cnt_ref, carry_ref):
    tm = x_ref.shape[0]

    @pl.when(pl.program_id(0) == 0)
    def _():
        carry_ref[...] = jnp.zeros_like(carry_ref)

    a = jnp.dot(oa_ref[...], wa_ref[...], preferred_element_type=f32)
    b = jnp.dot(on_ref[...], wb_ref[...], preferred_element_type=f32)
    ga = jax.nn.sigmoid(ga_ref[...].astype(f32))
    gb = jax.nn.sigmoid(gb_ref[...].astype(f32))
    merged = (ga * a + gb * b).astype(bf16)
    x1 = x_ref[...] + jnp.dot(merged, wo_ref[...], preferred_element_type=f32)
    x1_ref[...] = x1

    h2 = x1 * lax.rsqrt(jnp.mean(x1 * x1, axis=-1, keepdims=True) + EPS) * g2_ref[...]
    h_hi = h2.astype(bf16)
    h2_ref[...] = _pack_bf16_pairs(h_hi)
    h_lo = (h2 - h_hi.astype(f32)).astype(bf16)
    hw = jnp.dot(h_hi, wr_ref[...], preferred_element_type=f32)
    logits = (hw[:, :LANES] + hw[:, LANES:]
              + jnp.dot(h_lo, wr_ref[:, :LANES], preferred_element_type=f32)) + br_ref[...]

    lane = lax.broadcasted_iota(i32, logits.shape, 1)
    gl = jnp.where(lane < N_GROUPS, logits, NEG_INF)
    gmax = jnp.max(gl, axis=-1, keepdims=True)
    gsel = jnp.min(jnp.where(gl == gmax, lane, LANES), axis=-1, keepdims=True)
    gw = 1.0 / jnp.sum(jnp.exp(gl - gmax), axis=-1, keepdims=True)
    lo = ROUTE_LANE0 + gsel * EXPERTS_PER_GROUP
    el = jnp.where((lane >= lo) & (lane < lo + EXPERTS_PER_GROUP), logits, NEG_INF)
    m1 = jnp.max(el, axis=-1, keepdims=True)
    i1 = jnp.min(jnp.where(el == m1, lane, LANES), axis=-1, keepdims=True)
    el2 = jnp.where(lane == i1, NEG_INF, el)
    m2 = jnp.max(el2, axis=-1, keepdims=True)
    i2 = jnp.min(jnp.where(el2 == m2, lane, LANES), axis=-1, keepdims=True)
    r = jnp.exp(m2 - m1)
    gate1 = gw / (1.0 + r)
    gate2 = gw * r / (1.0 + r)

    oh = ((lane == i1) | (lane == i2)).astype(bf16)
    row = lax.broadcasted_iota(i32, (tm, tm), 0)
    col = lax.broadcasted_iota(i32, (tm, tm), 1)
    tri = (col < row).astype(bf16)
    before = carry_ref[...] + jnp.dot(tri, oh, preferred_element_type=f32)
    r1 = jnp.sum(jnp.where(lane == i1, before, 0.0), axis=-1, keepdims=True).astype(i32)
    r2 = jnp.sum(jnp.where(lane == i2, before, 0.0), axis=-1, keepdims=True).astype(i32)
    carry = carry_ref[...] + jnp.sum(oh.astype(f32), axis=0, keepdims=True)
    carry_ref[...] = carry
    cnt_ref[...] = carry

    ri_ref[...] = jnp.where(lane == 0, i1 - ROUTE_LANE0, jnp.where(lane == 1, i2 - ROUTE_LANE0,
                            jnp.where(lane == 2, r1, jnp.where(lane == 3, r2, 0))))
    rf_ref[...] = jnp.where(lane == 0, gate1, jnp.where(lane == 1, gate2, 0.0))


def _merge(oa, on, proj, x, wa, wb, wo, g2, wr, br, tm=256):
    n, d = x.shape
    tm = min(tm, n)
    aw = oa.shape[1]
    const = lambda shape: pl.BlockSpec(shape, lambda i: (0,) * len(shape), pipeline_mode=pl.Buffered(1))
    return pl.pallas_call(
        _merge_body,
        grid=(n // tm,),
        in_specs=[pl.BlockSpec((tm, aw), lambda i: (i, 0)),
                  pl.BlockSpec((tm, aw), lambda i: (i, 0)),
                  pl.BlockSpec((pl.Element(tm), pl.Element(d)), lambda i: (i * tm, GATE0 * HEAD_DIM)),
                  pl.BlockSpec((pl.Element(tm), pl.Element(d)), lambda i: (i * tm, GATE0 * HEAD_DIM + d)),
                  pl.BlockSpec((tm, d), lambda i: (i, 0)),
                  const((aw, d)), const((aw, d)), const((d, d)), const((1, d)),
                  const((d, 2 * LANES)), const((1, LANES))],
        out_specs=[pl.BlockSpec((tm, d), lambda i: (i, 0)),
                   pl.BlockSpec((tm, d // 2), lambda i: (i, 0)),
                   pl.BlockSpec((tm, LANES), lambda i: (i, 0)),
                   pl.BlockSpec((tm, LANES), lambda i: (i, 0)),
                   pl.BlockSpec((1, LANES), lambda i: (0, 0))],
        out_shape=[jax.ShapeDtypeStruct((n, d), f32), jax.ShapeDtypeStruct((n, d // 2), i32),
                   jax.ShapeDtypeStruct((n, LANES), i32), jax.ShapeDtypeStruct((n, LANES), f32),
                   jax.ShapeDtypeStruct((1, LANES), f32)],
        scratch_shapes=[pltpu.VMEM((1, LANES), f32)],
        compiler_params=_cparams(("arbitrary",)),
        name="merge_route",
    )(oa, on, proj, proj, x, wa, wb, wo, g2, wr, br)


def _moe_body(d1_ref, d2_ref, be_ref, bv_ref, nu_ref, h_hbm, wi_ref, wd_ref, y_hbm,
              xin0, xin1, yo0, yo1, wi_bf, wd_bf, enc_ref, gsem, ssem):
    b = pl.program_id(0)
    nu = nu_ref[0]
    n = d1_ref.shape[0]
    tb = xin0.shape[0]
    nb = be_ref.shape[0]
    xin, yo = (xin0, xin1), (yo0, yo1)

    def gather(blk, r, slot):
        enc = enc_ref[blk * tb + r]
        return pltpu.make_async_copy(h_hbm.at[pl.ds(enc & (n - 1), 1), :], xin[slot].at[pl.ds(r, 1), :],
                                     gsem.at[slot])

    def scatter(blk, r, slot):
        enc = enc_ref[blk * tb + r]
        return pltpu.make_async_copy(yo[slot].at[pl.ds(r, 1), :], y_hbm.at[pl.ds(enc, 1), :], ssem.at[slot])

    def for_rows(blk, fn):
        cnt = bv_ref[blk]
        groups = cnt // DMA_UNROLL

        def unrolled(g, c):
            for u in range(DMA_UNROLL):
                fn(g * DMA_UNROLL + u)
            return c

        def single(r, c):
            fn(r)
            return c

        lax.fori_loop(0, groups, unrolled, 0)
        lax.fori_loop(groups * DMA_UNROLL, cnt, single, 0)

    def for_rows_static(blk, fn):
        cnt = bv_ref[blk]
        for r in range(tb):
            pl.when(r < cnt)(functools.partial(fn, r))

    @pl.when(b == 0)
    def _():
        xin0[...] = jnp.zeros_like(xin0)
        xin1[...] = jnp.zeros_like(xin1)

        def fill(t, c):
            enc_ref[d1_ref[t]] = t
            enc_ref[d2_ref[t]] = n + t
            return c

        lax.fori_loop(0, n, fill, 0, unroll=DMA_UNROLL)
        for_rows(0, lambda r: gather(0, r, 0).start())

    def step(p):
        nxt = jnp.minimum(b + 1, nb - 1)

        @pl.when(b >= 2)
        def _():
            for_rows(b - 2, lambda r: scatter(b - 2, r, p).wait())

        for_rows(b, lambda r: gather(b, r, p).wait())
        for_rows_static(nxt, lambda r: gather(nxt, r, 1 - p).start(priority=r % 2))
        x_lo, x_hi = _unpack_bf16_pairs(xin[p][...])
        x = jnp.concatenate([x_lo.astype(bf16), x_hi.astype(bf16)], axis=1)
        h = jnp.dot(x, wi_bf[...], preferred_element_type=f32)
        a, u = h[:, :D_EXPERT], h[:, D_EXPERT:]
        act = (a * jax.nn.sigmoid(a) * u).astype(bf16)
        y = jnp.dot(act, wd_bf[...], preferred_element_type=f32)
        yo[p][...] = _pack_bf16_pairs(y.astype(bf16))
        for_rows_static(b, lambda r: scatter(b, r, p).start(priority=r % 2))

        @pl.when(b == nu - 1)
        def _():
            for_rows(nxt, lambda r: gather(nxt, r, 1 - p).wait())
            for_rows(b, lambda r: scatter(b, r, p).wait())

            @pl.when(b >= 1)
            def _():
                for_rows(b - 1, lambda r: scatter(b - 1, r, 1 - p).wait())

    @pl.when((b < nu) & ((b == 0) | (be_ref[b] != be_ref[jnp.maximum(b - 1, 0)])))
    def _():
        wi_bf[...] = wi_ref[0].astype(bf16)
        wd_bf[...] = wd_ref[0].astype(bf16)

    for p in range(2):
        pl.when((b < nu) & (b % 2 == p))(functools.partial(step, p))


def _moe(d1, d2, blk_expert, blk_valid, n_used, h2, wi, wd):
    n, dp = h2.shape
    d = 2 * dp
    assert n & (n - 1) == 0, "token count must be a power of two (row index = enc & (n - 1))"
    tb = EXPERT_BLOCK
    nb = blk_expert.shape[0]
    return pl.pallas_call(
        _moe_body,
        grid_spec=pltpu.PrefetchScalarGridSpec(
            num_scalar_prefetch=5,
            grid=(nb,),
            in_specs=[pl.BlockSpec(memory_space=pl.ANY),
                      pl.BlockSpec((1, d, 2 * D_EXPERT), lambda b, d1, d2, be, bv, nu: (be[b], 0, 0)),
                      pl.BlockSpec((1, D_EXPERT, d), lambda b, d1, d2, be, bv, nu: (be[b], 0, 0))],
            out_specs=pl.BlockSpec(memory_space=pl.ANY),
            scratch_shapes=[pltpu.VMEM((tb, dp), i32), pltpu.VMEM((tb, dp), i32),
                            pltpu.VMEM((tb, dp), i32), pltpu.VMEM((tb, dp), i32),
                            pltpu.VMEM((d, 2 * D_EXPERT), bf16), pltpu.VMEM((D_EXPERT, d), bf16),
                            pltpu.SMEM((nb * tb,), i32),
                            pltpu.SemaphoreType.DMA((2,)), pltpu.SemaphoreType.DMA((2,))]),
        out_shape=jax.ShapeDtypeStruct((2 * n, dp), i32),
        compiler_params=_cparams(("arbitrary",)),
        name="moe_experts",
    )(d1, d2, blk_expert, blk_valid, n_used, h2, wi, wd)


def _final_body(y1_ref, y2_ref, x1_ref, rf_ref, gf_ref, o_ref):
    rf = rf_ref[...]
    y1 = jnp.concatenate(_unpack_bf16_pairs(y1_ref[...]), axis=1)
    y2 = jnp.concatenate(_unpack_bf16_pairs(y2_ref[...]), axis=1)
    x = x1_ref[...] + (y1 * rf[:, 0:1] + y2 * rf[:, 1:2])
    o_ref[...] = x * lax.rsqrt(jnp.mean(x * x, axis=-1, keepdims=True) + EPS) * gf_ref[...]


def _final(y, x1, rf, gf, tm=512):
    n, d = x1.shape
    tm = min(tm, n)
    return pl.pallas_call(
        _final_body,
        grid=(n // tm,),
        in_specs=[pl.BlockSpec((tm, d // 2), lambda i: (i, 0)),
                  pl.BlockSpec((tm, d // 2), lambda i: (i + n // tm, 0)),
                  pl.BlockSpec((tm, d), lambda i: (i, 0)),
                  pl.BlockSpec((tm, LANES), lambda i: (i, 0)),
                  pl.BlockSpec((1, d), lambda i: (0, 0))],
        out_specs=pl.BlockSpec((tm, d), lambda i: (i, 0)),
        out_shape=jax.ShapeDtypeStruct((n, d), f32),
        compiler_params=_cparams(("parallel",)),
        name="moe_final",
    )(y, y, x1, rf, gf)


def _rope_tables(seq):
    t = jnp.arange(seq, dtype=i32)
    pos = jnp.stack([t // GRID_W, t % GRID_W], axis=-1).astype(f32)
    half = HEAD_DIM // 2
    inv_freq = ROPE_THETA ** (-jnp.arange(0, half, 2, dtype=f32) / half)
    ang = pos[:, :, None] * inv_freq[None, None, :]
    ang = jnp.concatenate([ang, ang], axis=-1).reshape(seq, HEAD_DIM)
    lane = jnp.arange(HEAD_DIM, dtype=i32)
    sign = jnp.where((lane % half) < half // 2, -1.0, 1.0).astype(f32)
    return jnp.cos(ang), jnp.sin(ang) * sign


def _layer(x, p):
    batch, seq, d = x.shape
    n = batch * seq
    x2 = x.reshape(n, d)
    assert seq // GRID_W >= NA_BAND_ROWS and seq // Q_BLOCK >= 5, "sequence too short for the 5 block geometries"
    proj = _inproj(x2, p["norm1_g"], p["w_in"], p["rope_cos"], p["rope_sin"], p["q_norm_g"], p["k_norm_g"], seq)
    oa = _gqa(proj, batch, seq)
    on = _na(proj, p["na_bias"], batch, seq)
    x1, h2, ri, rf, cnt = _merge(oa, on, proj, x2, p["w_branch_a"], p["w_branch_b"], p["w_out"], p["norm2_g"],
                                 p["w_router"], p["b_router"])

    tb = EXPERT_BLOCK
    n_rows = 2 * n + N_EXPERTS * tb
    counts = cnt[0, ROUTE_LANE0:ROUTE_LANE0 + N_EXPERTS].astype(i32)
    padded = (counts + tb - 1) // tb * tb
    pad_end = jnp.cumsum(padded)
    pad_start = pad_end - padded
    eids = jnp.arange(N_EXPERTS, dtype=i32)

    def start_of(e):
        return jnp.sum(jnp.where(e[:, None] == eids[None, :], pad_start[None, :], 0), axis=1)

    d1 = start_of(ri[:, 0]) + ri[:, 2]
    d2 = start_of(ri[:, 1]) + ri[:, 3]
    nb = n_rows // tb
    blk_start = jnp.arange(nb, dtype=i32) * tb
    blk_expert = jnp.minimum(jnp.sum((pad_end[None, :] <= blk_start[:, None]).astype(i32), axis=1), N_EXPERTS - 1)
    of_blk = blk_expert[:, None] == eids[None, :]
    row_in_expert = blk_start - jnp.sum(jnp.where(of_blk, pad_start[None, :], 0), axis=1)
    blk_valid = jnp.clip(jnp.sum(jnp.where(of_blk, counts[None, :], 0), axis=1) - row_in_expert, 0, tb)
    blk_valid = jnp.where(blk_start < pad_end[-1], blk_valid, 0).astype(i32)
    n_used = (pad_end[-1:] // tb).astype(i32)

    y = _moe(d1, d2, blk_expert, blk_valid, n_used, h2, p["w_exp_in"], p["w_exp_down"])
    out = _final(y, x1, rf, p["norm_f_g"])
    return out.reshape(batch, seq, d)


def kernel(x_prompt, x_sample, norm1_g, w_in, q_norm_g, k_norm_g, na_rpb, w_branch_a, w_branch_b, w_out, norm2_g,
           w_router_group, b_router_group, w_router_expert, b_router_expert, w_exp_in, w_exp_down, norm_f_g):
    assert norm1_g.shape[0] == 1, "one encoder layer"
    w_r = jnp.zeros((D_MODEL, LANES), f32)
    w_r = w_r.at[:, :N_GROUPS].set(w_router_group[0]).at[:, ROUTE_LANE0:ROUTE_LANE0 + N_EXPERTS].set(w_router_expert[0])
    b_r = jnp.zeros((1, LANES), f32)
    b_r = b_r.at[0, :N_GROUPS].set(b_router_group[0]).at[0, ROUTE_LANE0:ROUTE_LANE0 + N_EXPERTS].set(b_router_expert[0])
    w_r_hi = w_r.astype(bf16)
    p = {
        "norm1_g": norm1_g[0][None], "w_in": w_in[0].astype(bf16),
        "q_norm_g": q_norm_g[0][None], "k_norm_g": k_norm_g[0][None],
        "w_branch_a": w_branch_a[0].astype(bf16), "w_branch_b": w_branch_b[0].astype(bf16),
        "w_out": w_out[0].astype(bf16), "norm2_g": norm2_g[0][None],
        "w_router": jnp.concatenate([w_r_hi, (w_r - w_r_hi.astype(f32)).astype(bf16)], axis=1), "b_router": b_r,
        "w_exp_in": w_exp_in[0], "w_exp_down": w_exp_down[0],
        "norm_f_g": norm_f_g[None],
    }
    max_seq = max(x_prompt.shape[1], x_sample.shape[1])
    p["rope_cos"], p["rope_sin"] = _rope_tables(max_seq)
    p["na_bias"] = _na_bias_tables(na_rpb[0], max_seq // GRID_W)
    return _layer(x_prompt, p), _layer(x_sample, p)
```

```python
import functools

import jax
import jax.numpy as jnp
from jax import lax
from jax.experimental import pallas as pl
from jax.experimental.pallas import tpu as pltpu

f32 = jnp.float32
bf16 = jnp.bfloat16
i32 = jnp.int32

D_MODEL = 2048
GRID_W = 64
HEAD_DIM = 128
A_HEADS = 8
A_KV_HEADS = 2
A_GROUP = A_HEADS // A_KV_HEADS
NA_HEADS = 8
NA_WIN_ROWS = 8
NA_WIN_COLS = 16
Q_BLOCK = 128
ROPE_THETA = 10000.0
N_GROUPS = 4
EXPERTS_PER_GROUP = 8
N_EXPERTS = N_GROUPS * EXPERTS_PER_GROUP
D_EXPERT = D_MODEL // 4
EPS = 1e-6
NEG_INF = -1e30
SCALE = HEAD_DIM ** -0.5
LOG2E = 1.4426950408889634

QA0, KA0, VA0, QN0, KN0, VN0, GATE0 = 0, 8, 10, 12, 20, 28, 36
IN_COLS = (GATE0 + 2 * D_MODEL // HEAD_DIM) * HEAD_DIM

LANES = 128
NA_BAND_ROWS = 10
NA_KEYS = NA_BAND_ROWS * GRID_W
NA_PAIR_ROWS = NA_BAND_ROWS + Q_BLOCK // GRID_W
ROUTE_LANE0 = 4
EXPERT_BLOCK = 256
DMA_UNROLL = 8
DMA_GROUP = 32
VMEM_LIMIT = 56 * 1024 * 1024


def _cparams(sem):
    return pltpu.CompilerParams(dimension_semantics=sem, vmem_limit_bytes=VMEM_LIMIT)


def _inproj_body(x_ref, g_ref, w_ref, cos_ref, sin_ref, qg_ref, kg_ref, o_ref, h_ref):
    j = pl.program_id(1)

    @pl.when(j == 0)
    def _():
        x = x_ref[...]
        ms = jnp.mean(x * x, axis=-1, keepdims=True)
        h_ref[...] = (x * lax.rsqrt(ms + EPS) * g_ref[...]).astype(bf16)

    @pl.when(j != 0)
    def _():
        o_ref[...] = jnp.dot(h_ref[...], w_ref[...], preferred_element_type=f32).astype(o_ref.dtype)

    @pl.when(j == 0)
    def _():
        cos = cos_ref[...]
        sin = sin_ref[...]
        lane = lax.broadcasted_iota(i32, cos.shape, 1)
        first = (lane % (HEAD_DIM // 2)) < (HEAD_DIM // 4)

        def prep(x, g, scale):
            y = x * lax.rsqrt(jnp.mean(x * x, axis=-1, keepdims=True) + EPS) * g
            rot = jnp.where(first, pltpu.roll(y, HEAD_DIM - HEAD_DIM // 4, 1), pltpu.roll(y, HEAD_DIM // 4, 1))
            return (y * cos + rot * sin) * scale

        h_in = h_ref[...]
        for h0 in range(0, A_HEADS + A_KV_HEADS, 2):
            acc = jnp.dot(h_in, w_ref[:, h0 * HEAD_DIM:(h0 + 2) * HEAD_DIM], preferred_element_type=f32)
            for h in (h0, h0 + 1):
                is_q = h < A_HEADS
                seg = acc[:, (h - h0) * HEAD_DIM:(h - h0 + 1) * HEAD_DIM]
                o_ref[:, h * HEAD_DIM:(h + 1) * HEAD_DIM] = prep(
                    seg, (qg_ref if is_q else kg_ref)[...], SCALE * LOG2E if is_q else 1.0).astype(o_ref.dtype)
        rest = (A_HEADS + A_KV_HEADS) * HEAD_DIM
        o_ref[:, rest:] = jnp.dot(h_in, w_ref[:, rest:], preferred_element_type=f32).astype(o_ref.dtype)


def _inproj(x, g, w, cos, sin, qg, kg, seq, tm=1024, tn=IN_COLS // 4):
    n, d = x.shape
    c = w.shape[1]
    tm = min(tm, seq)
    spb = seq // tm
    assert QA0 == 0 and KA0 == A_HEADS and tn >= (A_HEADS + A_KV_HEADS) * HEAD_DIM
    return pl.pallas_call(
        _inproj_body,
        grid=(n // tm, c // tn),
        in_specs=[pl.BlockSpec((tm, d), lambda i, j: (i, 0)),
                  pl.BlockSpec((1, d), lambda i, j: (0, 0)),
                  pl.BlockSpec((d, tn), lambda i, j: (0, j)),
                  pl.BlockSpec((tm, HEAD_DIM), lambda i, j: (i % spb, 0)),
                  pl.BlockSpec((tm, HEAD_DIM), lambda i, j: (i % spb, 0)),
                  pl.BlockSpec((1, HEAD_DIM), lambda i, j: (0, 0)),
                  pl.BlockSpec((1, HEAD_DIM), lambda i, j: (0, 0))],
        out_specs=pl.BlockSpec((tm, tn), lambda i, j: (i, j)),
        out_shape=jax.ShapeDtypeStruct((n, c), bf16),
        scratch_shapes=[pltpu.VMEM((tm, d), bf16)],
        compiler_params=_cparams(("parallel", "arbitrary")),
        name="inproj",
    )(x, g, w, cos, sin, qg, kg)


def _lane_tile(x, n):
    return jnp.concatenate([x] * n, axis=1)


def _gqa_body(q_ref, k_ref, v_ref, o_ref, m_ref, acc_ref, *, tk):
    nkv = k_ref.shape[0] // tk
    m_ref[...] = jnp.full(m_ref.shape, NEG_INF, f32)
    acc_ref[...] = jnp.zeros(acc_ref.shape, f32)
    ones = jnp.ones((tk, HEAD_DIM), bf16)

    def step(j, c):
        off = pl.multiple_of(j * tk, tk)
        k = k_ref[pl.ds(off, tk), :]
        v1 = jnp.concatenate([v_ref[pl.ds(off, tk), :], ones], axis=1)

        def scores(g):
            q = q_ref[:, g * HEAD_DIM:(g + 1) * HEAD_DIM]
            return lax.dot_general(q, k, (((1,), (1,)), ((), ())), preferred_element_type=f32)

        s = scores(0)
        for g in range(A_GROUP):
            s_next = scores(g + 1) if g + 1 < A_GROUP else None
            m = m_ref[g]
            m_new = jnp.maximum(m, jnp.max(s, axis=-1, keepdims=True))
            alpha = jnp.exp2(m - m_new)
            p = jnp.exp2(s - _lane_tile(m_new, tk // LANES))
            pv = jnp.dot(p.astype(bf16), v1, preferred_element_type=f32)
            acc_ref[g] = _lane_tile(alpha, 2) * acc_ref[g] + pv
            m_ref[g] = m_new
            s = s_next
        return c

    lax.fori_loop(0, nkv, step, 0)
    for g in range(A_GROUP):
        acc = acc_ref[g]
        o_ref[:, g * HEAD_DIM:(g + 1) * HEAD_DIM] = (acc[:, :HEAD_DIM] / acc[:, HEAD_DIM:]).astype(o_ref.dtype)


def _gqa(proj, batch, seq, tq=1024, tk=2048):
    n = proj.shape[0]
    tq, tk = min(tq, seq), min(tk, seq)
    nq = seq // tq
    gw = A_GROUP * HEAD_DIM
    return pl.pallas_call(
        functools.partial(_gqa_body, tk=tk),
        grid=(batch, A_KV_HEADS, nq),
        in_specs=[pl.BlockSpec((tq, gw), lambda b, h, i: (b * nq + i, QA0 // A_GROUP + h)),
                  pl.BlockSpec((seq, HEAD_DIM), lambda b, h, i: (b, KA0 + h)),
                  pl.BlockSpec((seq, HEAD_DIM), lambda b, h, i: (b, VA0 + h))],
        out_specs=pl.BlockSpec((tq, gw), lambda b, h, i: (b * nq + i, h)),
        out_shape=jax.ShapeDtypeStruct((n, A_HEADS * HEAD_DIM), bf16),
        scratch_shapes=[pltpu.VMEM((A_GROUP, tq, LANES), f32), pltpu.VMEM((A_GROUP, tq, 2 * HEAD_DIM), f32)],
        compiler_params=_cparams(("parallel", "parallel", "arbitrary")),
        name="gqa",
    )(proj, proj, proj)


def _na_bias_tables(rpb, rows):
    nblk = rows * GRID_W // Q_BLOCK
    rpq = Q_BLOCK // GRID_W
    ncls = 5
    blk = jnp.array([0, 1, 2, nblk - 2, nblk - 1], i32)
    col = jnp.arange(GRID_W, dtype=i32)
    col_start = jnp.clip(col - NA_WIN_COLS // 2, 0, GRID_W - NA_WIN_COLS)
    col_in = (col[None, :] >= col_start[:, None]) & (col[None, :] < col_start[:, None] + NA_WIN_COLS)
    dc = jnp.clip(col[None, :] - col[:, None] + NA_WIN_COLS - 1, 0, 2 * NA_WIN_COLS - 2)
    r0 = blk * rpq
    bs = jnp.clip(r0 - NA_WIN_ROWS // 2, 0, rows - NA_BAND_ROWS)
    q_row = r0[:, None] + jnp.arange(rpq, dtype=i32)[None, :]
    k_row = bs[:, None] + jnp.arange(NA_BAND_ROWS, dtype=i32)[None, :]
    row_start = jnp.clip(q_row - NA_WIN_ROWS // 2, 0, rows - NA_WIN_ROWS)
    row_in = (k_row[:, None, :] >= row_start[:, :, None]) & (k_row[:, None, :] < row_start[:, :, None] + NA_WIN_ROWS)
    dr = jnp.clip(k_row[:, None, :] - q_row[:, :, None] + NA_WIN_ROWS - 1, 0, 2 * NA_WIN_ROWS - 2)
    picked = rpb.astype(f32)[:, dr, :]
    onehot = (dc[None] == jnp.arange(2 * NA_WIN_COLS - 1, dtype=i32)[:, None, None]).astype(f32)
    bias = jnp.einsum("hcqkd,dxy->chqxky", picked, onehot, precision=lax.Precision.HIGHEST)
    mask = row_in[:, None, :, None, :, None] & col_in[None, None, None, :, None, :]
    bias = jnp.where(mask, bias * LOG2E, NEG_INF)
    return bias.reshape(ncls, NA_HEADS, Q_BLOCK, NA_KEYS)


def _na_band_row(first_q_row, rows, band_rows):
    return jnp.clip(first_q_row - NA_WIN_ROWS // 2, 0, rows - band_rows)


def _na_body(q_ref, k_ref, v_ref, tab0_ref, tab1_ref, o_ref, *, rows):
    nh = q_ref.shape[1] // HEAD_DIM
    rpq = Q_BLOCK // GRID_W
    first_q_row = pl.program_id(1) * (2 * rpq)
    win_row = _na_band_row(first_q_row, rows, NA_PAIR_ROWS)
    tabs = (tab0_ref, tab1_ref)
    offs = [pl.multiple_of((_na_band_row(first_q_row + j * rpq, rows, NA_BAND_ROWS) - win_row) * GRID_W, GRID_W)
            for j in range(2)]
    ones = jnp.ones((NA_KEYS, HEAD_DIM), bf16)
    units = [(j, h) for j in range(2) for h in range(nh)]

    def scores(j, h):
        sl = slice(h * HEAD_DIM, (h + 1) * HEAD_DIM)
        q = q_ref[j * Q_BLOCK:(j + 1) * Q_BLOCK, sl]
        k = k_ref[pl.ds(offs[j], NA_KEYS), sl]
        s = lax.dot_general(q, k, (((1,), (1,)), ((), ())), preferred_element_type=f32)
        return s * (SCALE * LOG2E) + tabs[j][0, h]

    s = scores(*units[0])
    for u, (j, h) in enumerate(units):
        sl = slice(h * HEAD_DIM, (h + 1) * HEAD_DIM)
        s_next = scores(*units[u + 1]) if u + 1 < len(units) else None
        m = jnp.broadcast_to(jnp.max(s, axis=-1, keepdims=True), (Q_BLOCK, LANES))
        p = jnp.exp2(s - _lane_tile(m, NA_KEYS // LANES))
        v1 = jnp.concatenate([v_ref[pl.ds(offs[j], NA_KEYS), sl], ones], axis=1)
        pv = jnp.dot(p.astype(bf16), v1, preferred_element_type=f32)
        o_ref[j * Q_BLOCK:(j + 1) * Q_BLOCK, sl] = (pv[:, :HEAD_DIM] / pv[:, HEAD_DIM:]).astype(o_ref.dtype)
        s = s_next


def _na(proj, tab, batch, seq):
    n = proj.shape[0]
    nblk = seq // Q_BLOCK
    rows = seq // GRID_W
    rpq = Q_BLOCK // GRID_W
    w = NA_HEADS * HEAD_DIM
    assert nblk % 2 == 0 and rows >= NA_PAIR_ROWS
    npair = nblk // 2
    win_keys = NA_PAIR_ROWS * GRID_W

    def window(b, i):
        return pl.multiple_of(b * seq + _na_band_row(i * 2 * rpq, rows, NA_PAIR_ROWS) * GRID_W, GRID_W)

    def cls(blk):
        return jnp.where(blk < 2, blk, jnp.where(blk <= nblk - 3, 2, blk - (nblk - 5)))

    return pl.pallas_call(
        functools.partial(_na_body, rows=rows),
        grid=(batch, npair),
        in_specs=[pl.BlockSpec((pl.Element(2 * Q_BLOCK), pl.Element(w)),
                               lambda b, i: ((b * npair + i) * 2 * Q_BLOCK, QN0 * HEAD_DIM)),
                  pl.BlockSpec((pl.Element(win_keys), pl.Element(w)), lambda b, i: (window(b, i), KN0 * HEAD_DIM)),
                  pl.BlockSpec((pl.Element(win_keys), pl.Element(w)), lambda b, i: (window(b, i), VN0 * HEAD_DIM)),
                  pl.BlockSpec((1, NA_HEADS, Q_BLOCK, NA_KEYS), lambda b, i: (cls(2 * i), 0, 0, 0)),
                  pl.BlockSpec((1, NA_HEADS, Q_BLOCK, NA_KEYS), lambda b, i: (cls(2 * i + 1), 0, 0, 0))],
        out_specs=pl.BlockSpec((2 * Q_BLOCK, w), lambda b, i: (b * npair + i, 0)),
        out_shape=jax.ShapeDtypeStruct((n, w), bf16),
        compiler_params=_cparams(("parallel", "arbitrary")),
        name="natten",
    )(proj, proj, proj, tab, tab)


def _pack_bf16_pairs(x):
    c = x.shape[1] // 2
    bits = lax.bitcast_convert_type(x.astype(f32), i32)
    return lax.shift_right_logical(bits[:, :c], 16) | bits[:, c:]


def _unpack_bf16_pairs(packed):
    lo = lax.bitcast_convert_type(lax.shift_left(packed, 16), f32)
    hi = lax.bitcast_convert_type(packed & jnp.int32(-65536), f32)
    return lo, hi


def _merge_body(oa_ref, on_ref, ga_ref, gb_ref, x_ref, wa_ref, wb_ref, wo_ref, g2_ref, wr_ref, br_ref,
                x1_ref, h2_ref, ri_ref, rf_ref, cnt_ref, carry_ref):
    tm = x_ref.shape[0]

    @pl.when(pl.program_id(0) == 0)
    def _():
        carry_ref[...] = jnp.zeros_like(carry_ref)

    a = jnp.dot(oa_ref[...], wa_ref[...], preferred_element_type=f32)
    b = jnp.dot(on_ref[...], wb_ref[...], preferred_element_type=f32)
    ga = jax.nn.sigmoid(ga_ref[...].astype(f32))
    gb = jax.nn.sigmoid(gb_ref[...].astype(f32))
    merged = (ga * a + gb * b).astype(bf16)
    x1 = x_ref[...] + jnp.dot(merged, wo_ref[...], preferred_element_type=f32)
    x1_ref[...] = x1

    h2 = x1 * lax.rsqrt(jnp.mean(x1 * x1, axis=-1, keepdims=True) + EPS) * g2_ref[...]
    h_hi = h2.astype(bf16)
    h2_ref[...] = _pack_bf16_pairs(h_hi)
    h_lo = (h2 - h_hi.astype(f32)).astype(bf16)
    hw = jnp.dot(h_hi, wr_ref[...], preferred_element_type=f32)
    logits = (hw[:, :LANES] + hw[:, LANES:]
              + jnp.dot(h_lo, wr_ref[:, :LANES], preferred_element_type=f32)) + br_ref[...]

    lane = lax.broadcasted_iota(i32, logits.shape, 1)
    gl = jnp.where(lane < N_GROUPS, logits, NEG_INF)
    gmax = jnp.max(gl, axis=-1, keepdims=True)
    gsel = jnp.min(jnp.where(gl == gmax, lane, LANES), axis=-1, keepdims=True)
    gw = 1.0 / jnp.sum(jnp.exp(gl - gmax), axis=-1, keepdims=True)
    lo = ROUTE_LANE0 + gsel * EXPERTS_PER_GROUP
    el = jnp.where((lane >= lo) & (lane < lo + EXPERTS_PER_GROUP), logits, NEG_INF)
    m1 = jnp.max(el, axis=-1, keepdims=True)
    i1 = jnp.min(jnp.where(el == m1, lane, LANES), axis=-1, keepdims=True)
    el2 = jnp.where(lane == i1, NEG_INF, el)
    m2 = jnp.max(el2, axis=-1, keepdims=True)
    i2 = jnp.min(jnp.where(el2 == m2, lane, LANES), axis=-1, keepdims=True)
    r = jnp.exp(m2 - m1)
    gate1 = gw / (1.0 + r)
    gate2 = gw * r / (1.0 + r)

    oh = ((lane == i1) | (lane == i2)).astype(bf16)
    row = lax.broadcasted_iota(i32, (tm, tm), 0)
    col = lax.broadcasted_iota(i32, (tm, tm), 1)
    tri = (col < row).astype(bf16)
    before = carry_ref[...] + jnp.dot(tri, oh, preferred_element_type=f32)
    r1 = jnp.sum(jnp.where(lane == i1, before, 0.0), axis=-1, keepdims=True).astype(i32)
    r2 = jnp.sum(jnp.where(lane == i2, before, 0.0), axis=-1, keepdims=True).astype(i32)
    carry = carry_ref[...] + jnp.sum(oh.astype(f32), axis=0, keepdims=True)
    carry_ref[...] = carry
    cnt_ref[...] = carry

    ri_ref[...] = jnp.where(lane == 0, i1 - ROUTE_LANE0, jnp.where(lane == 1, i2 - ROUTE_LANE0,
                            jnp.where(lane == 2, r1, jnp.where(lane == 3, r2, 0))))
    rf_ref[...] = jnp.where(lane == 0, gate1, jnp.where(lane == 1, gate2, 0.0))


def _merge(oa, on, proj, x, wa, wb, wo, g2, wr, br, tm=256):
    n, d = x.shape
    tm = min(tm, n)
    aw = oa.shape[1]
    const = lambda shape: pl.BlockSpec(shape, lambda i: (0,) * len(shape), pipeline_mode=pl.Buffered(1))
    return pl.pallas_call(
        _merge_body,
        grid=(n // tm,),
        in_specs=[pl.BlockSpec((tm, aw), lambda i: (i, 0)),
                  pl.BlockSpec((tm, aw), lambda i: (i, 0)),
                  pl.BlockSpec((pl.Element(tm), pl.Element(d)), lambda i: (i * tm, GATE0 * HEAD_DIM)),
                  pl.BlockSpec((pl.Element(tm), pl.Element(d)), lambda i: (i * tm, GATE0 * HEAD_DIM + d)),
                  pl.BlockSpec((tm, d), lambda i: (i, 0)),
                  const((aw, d)), const((aw, d)), const((d, d)), const((1, d)),
                  const((d, 2 * LANES)), const((1, LANES))],
        out_specs=[pl.BlockSpec((tm, d), lambda i: (i, 0)),
                   pl.BlockSpec((tm, d // 2), lambda i: (i, 0)),
                   pl.BlockSpec((tm, LANES), lambda i: (i, 0)),
                   pl.BlockSpec((tm, LANES), lambda i: (i, 0)),
                   pl.BlockSpec((1, LANES), lambda i: (0, 0))],
        out_shape=[jax.ShapeDtypeStruct((n, d), f32), jax.ShapeDtypeStruct((n, d // 2), i32),
                   jax.ShapeDtypeStruct((n, LANES), i32), jax.ShapeDtypeStruct((n, LANES), f32),
                   jax.ShapeDtypeStruct((1, LANES), f32)],
        scratch_shapes=[pltpu.VMEM((1, LANES), f32)],
        compiler_params=_cparams(("arbitrary",)),
        name="merge_route",
    )(oa, on, proj, proj, x, wa, wb, wo, g2, wr, br)


def _moe_body(d1_ref, d2_ref, be_ref, bv_ref, nu_ref, h_hbm, wi_ref, wd_ref, y_hbm,
              xin0, xin1, yo0, yo1, wi_bf, wd_bf, enc_ref, gsem, ssem):
    b = pl.program_id(0)
    nu = nu_ref[0]
    n = d1_ref.shape[0]
    tb = xin0.shape[0]
    nb = be_ref.shape[0]
    xin, yo = (xin0, xin1), (yo0, yo1)

    def gather(blk, r, slot):
        enc = enc_ref[blk * tb + r]
        return pltpu.make_async_copy(h_hbm.at[pl.ds(enc & (n - 1), 1), :], xin[slot].at[pl.ds(r, 1), :],
                                     gsem.at[slot])

    def scatter(blk, r, slot):
        enc = enc_ref[blk * tb + r]
        return pltpu.make_async_copy(yo[slot].at[pl.ds(r, 1), :], y_hbm.at[pl.ds(enc, 1), :], ssem.at[slot])

    def for_rows(blk, fn):
        cnt = bv_ref[blk]
        groups = cnt // DMA_UNROLL

        def unrolled(g, c):
            for u in range(DMA_UNROLL):
                fn(g * DMA_UNROLL + u)
            return c

        def single(r, c):
            fn(r)
            return c

        lax.fori_loop(0, groups, unrolled, 0)
        lax.fori_loop(groups * DMA_UNROLL, cnt, single, 0)

    def start_rows(blk, copy):
        cnt = bv_ref[blk]
        for g in range(tb // DMA_GROUP):
            @pl.when((g + 1) * DMA_GROUP <= cnt)
            def _():
                for r in range(g * DMA_GROUP, (g + 1) * DMA_GROUP):
                    copy(r).start(priority=r % 2)

        def single(r, c):
            copy(r).start()
            return c

        lax.fori_loop(cnt // DMA_GROUP * DMA_GROUP, cnt, single, 0)

    @pl.when(b == 0)
    def _():
        xin0[...] = jnp.zeros_like(xin0)
        xin1[...] = jnp.zeros_like(xin1)

        def fill(t, c):
            enc_ref[d1_ref[t]] = t
            enc_ref[d2_ref[t]] = n + t
            return c

        lax.fori_loop(0, n, fill, 0, unroll=DMA_UNROLL)
        for_rows(0, lambda r: gather(0, r, 0).start())

    def step(p):
        nxt = jnp.minimum(b + 1, nb - 1)

        @pl.when(b >= 2)
        def _():
            for_rows(b - 2, lambda r: scatter(b - 2, r, p).wait())

        for_rows(b, lambda r: gather(b, r, p).wait())
        start_rows(nxt, lambda r: gather(nxt, r, 1 - p))
        x_lo, x_hi = _unpack_bf16_pairs(xin[p][...])
        x = jnp.concatenate([x_lo.astype(bf16), x_hi.astype(bf16)], axis=1)
        h = jnp.dot(x, wi_bf[...], preferred_element_type=f32)
        a, u = h[:, :D_EXPERT], h[:, D_EXPERT:]
        act = (a * jax.nn.sigmoid(a) * u).astype(bf16)
        y = jnp.dot(act, wd_bf[...], preferred_element_type=f32)
        yo[p][...] = _pack_bf16_pairs(y.astype(bf16))
        start_rows(b, lambda r: scatter(b, r, p))

        @pl.when(b == nu - 1)
        def _():
            for_rows(nxt, lambda r: gather(nxt, r, 1 - p).wait())
            for_rows(b, lambda r: scatter(b, r, p).wait())

            @pl.when(b >= 1)
            def _():
                for_rows(b - 1, lambda r: scatter(b - 1, r, 1 - p).wait())

    @pl.when((b < nu) & ((b == 0) | (be_ref[b] != be_ref[jnp.maximum(b - 1, 0)])))
    def _():
        wi_bf[...] = wi_ref[0].astype(bf16)
        wd_bf[...] = wd_ref[0].astype(bf16)

    for p in range(2):
        pl.when((b < nu) & (b % 2 == p))(functools.partial(step, p))


def _moe(d1, d2, blk_expert, blk_valid, n_used, h2, wi, wd):
    n, dp = h2.shape
    d = 2 * dp
    assert n & (n - 1) == 0, "token count must be a power of two (row index = enc & (n - 1))"
    tb = EXPERT_BLOCK
    nb = blk_expert.shape[0]
    return pl.pallas_call(
        _moe_body,
        grid_spec=pltpu.PrefetchScalarGridSpec(
            num_scalar_prefetch=5,
            grid=(nb,),
            in_specs=[pl.BlockSpec(memory_space=pl.ANY),
                      pl.BlockSpec((1, d, 2 * D_EXPERT), lambda b, d1, d2, be, bv, nu: (be[b], 0, 0)),
                      pl.BlockSpec((1, D_EXPERT, d), lambda b, d1, d2, be, bv, nu: (be[b], 0, 0))],
            out_specs=pl.BlockSpec(memory_space=pl.ANY),
            scratch_shapes=[pltpu.VMEM((tb, dp), i32), pltpu.VMEM((tb, dp), i32),
                            pltpu.VMEM((tb, dp), i32), pltpu.VMEM((tb, dp), i32),
                            pltpu.VMEM((d, 2 * D_EXPERT), bf16), pltpu.VMEM((D_EXPERT, d), bf16),
                            pltpu.SMEM((nb * tb,), i32),
                            pltpu.SemaphoreType.DMA((2,)), pltpu.SemaphoreType.DMA((2,))]),
        out_shape=jax.ShapeDtypeStruct((2 * n, dp), i32),
        compiler_params=_cparams(("arbitrary",)),
        name="moe_experts",
    )(d1, d2, blk_expert, blk_valid, n_used, h2, wi, wd)


def _final_body(y1_ref, y2_ref, x1_ref, rf_ref, gf_ref, o_ref):
    rf = rf_ref[...]
    y1 = jnp.concatenate(_unpack_bf16_pairs(y1_ref[...]), axis=1)
    y2 = jnp.concatenate(_unpack_bf16_pairs(y2_ref[...]), axis=1)
    x = x1_ref[...] + (y1 * rf[:, 0:1] + y2 * rf[:, 1:2])
    o_ref[...] = x * lax.rsqrt(jnp.mean(x * x, axis=-1, keepdims=True) + EPS) * gf_ref[...]


def _final(y, x1, rf, gf, tm=512):
    n, d = x1.shape
    tm = min(tm, n)
    return pl.pallas_call(
        _final_body,
        grid=(n // tm,),
        in_specs=[pl.BlockSpec((tm, d // 2), lambda i: (i, 0)),
                  pl.BlockSpec((tm, d // 2), lambda i: (i + n // tm, 0)),
                  pl.BlockSpec((tm, d), lambda i: (i, 0)),
                  pl.BlockSpec((tm, LANES), lambda i: (i, 0)),
                  pl.BlockSpec((1, d), lambda i: (0, 0))],
        out_specs=pl.BlockSpec((tm, d), lambda i: (i, 0)),
        out_shape=jax.ShapeDtypeStruct((n, d), f32),
        compiler_params=_cparams(("parallel",)),
        name="moe_final",
    )(y, y, x1, rf, gf)


def _rope_tables(seq):
    rows = seq // GRID_W
    half = HEAD_DIM // 2
    inv_freq = ROPE_THETA ** (-jnp.arange(0, half, 2, dtype=f32) / half)
    inv_freq = jnp.concatenate([inv_freq, inv_freq])
    sign = jnp.where(jnp.arange(half) < half // 2, -1.0, 1.0).astype(f32)

    def table(fn, fold):
        by_row = fn(jnp.arange(rows, dtype=f32)[:, None] * inv_freq[None, :]) * fold
        by_col = fn(jnp.arange(GRID_W, dtype=f32)[:, None] * inv_freq[None, :]) * fold
        full = jnp.concatenate([jnp.broadcast_to(by_row[:, None, :], (rows, GRID_W, half)),
                                jnp.broadcast_to(by_col[None, :, :], (rows, GRID_W, half))], axis=-1)
        return full.reshape(seq, HEAD_DIM)

    return table(jnp.cos, 1.0), table(jnp.sin, sign)


def _layer(x, p):
    batch, seq, d = x.shape
    n = batch * seq
    x2 = x.reshape(n, d)
    assert seq // GRID_W >= NA_BAND_ROWS and seq // Q_BLOCK >= 5, "sequence too short for the 5 block geometries"
    proj = _inproj(x2, p["norm1_g"], p["w_in"], p["rope_cos"], p["rope_sin"], p["q_norm_g"], p["k_norm_g"], seq)
    oa = _gqa(proj, batch, seq)
    on = _na(proj, p["na_bias"], batch, seq)
    x1, h2, ri, rf, cnt = _merge(oa, on, proj, x2, p["w_branch_a"], p["w_branch_b"], p["w_out"], p["norm2_g"],
                                 p["w_router"], p["b_router"])

    tb = EXPERT_BLOCK
    n_rows = 2 * n + N_EXPERTS * tb
    counts = cnt[0, ROUTE_LANE0:ROUTE_LANE0 + N_EXPERTS].astype(i32)
    padded = (counts + tb - 1) // tb * tb
    pad_end = jnp.cumsum(padded)
    pad_start = pad_end - padded
    eids = jnp.arange(N_EXPERTS, dtype=i32)

    def start_of(e):
        return jnp.sum(jnp.where(e[:, None] == eids[None, :], pad_start[None, :], 0), axis=1)

    d1 = start_of(ri[:, 0]) + ri[:, 2]
    d2 = start_of(ri[:, 1]) + ri[:, 3]
    nb = n_rows // tb
    blk_start = jnp.arange(nb, dtype=i32) * tb
    blk_expert = jnp.minimum(jnp.sum((pad_end[None, :] <= blk_start[:, None]).astype(i32), axis=1), N_EXPERTS - 1)
    of_blk = blk_expert[:, None] == eids[None, :]
    row_in_expert = blk_start - jnp.sum(jnp.where(of_blk, pad_start[None, :], 0), axis=1)
    blk_valid = jnp.clip(jnp.sum(jnp.where(of_blk, counts[None, :], 0), axis=1) - row_in_expert, 0, tb)
    blk_valid = jnp.where(blk_start < pad_end[-1], blk_valid, 0).astype(i32)
    n_used = (pad_end[-1:] // tb).astype(i32)

    y = _moe(d1, d2, blk_expert, blk_valid, n_used, h2, p["w_exp_in"], p["w_exp_down"])
    out = _final(y, x1, rf, p["norm_f_g"])
    return out.reshape(batch, seq, d)


def kernel(x_prompt, x_sample, norm1_g, w_in, q_norm_g, k_norm_g, na_rpb, w_branch_a, w_branch_b, w_out, norm2_g,
           w_router_group, b_router_group, w_router_expert, b_router_expert, w_exp_in, w_exp_down, norm_f_g):
    assert norm1_g.shape[0] == 1, "one encoder layer"
    w_r = jnp.zeros((D_MODEL, LANES), f32)
    w_r = w_r.at[:, :N_GROUPS].set(w_router_group[0]).at[:, ROUTE_LANE0:ROUTE_LANE0 + N_EXPERTS].set(w_router_expert[0])
    b_r = jnp.zeros((1, LANES), f32)
    b_r = b_r.at[0, :N_GROUPS].set(b_router_group[0]).at[0, ROUTE_LANE0:ROUTE_LANE0 + N_EXPERTS].set(b_router_expert[0])
    w_r_hi = w_r.astype(bf16)
    p = {
        "norm1_g": norm1_g[0][None], "w_in": w_in[0].astype(bf16),
        "q_norm_g": q_norm_g[0][None], "k_norm_g": k_norm_g[0][None],
        "w_branch_a": w_branch_a[0].astype(bf16), "w_branch_b": w_branch_b[0].astype(bf16),
        "w_out": w_out[0].astype(bf16), "norm2_g": norm2_g[0][None],
        "w_router": jnp.concatenate([w_r_hi, (w_r - w_r_hi.astype(f32)).astype(bf16)], axis=1), "b_router": b_r,
        "w_exp_in": w_exp_in[0], "w_exp_down": w_exp_down[0],
        "norm_f_g": norm_f_g[None],
    }
    max_seq = max(x_prompt.shape[1], x_sample.shape[1])
    p["rope_cos"], p["rope_sin"] = _rope_tables(max_seq)
    p["na_bias"] = _na_bias_tables(na_rpb[0], max_seq // GRID_W)
    return _layer(x_prompt, p), _layer(x_sample, p)
```

```python
import functools

import jax
import jax.numpy as jnp
from jax import lax
from jax.experimental import pallas as pl
from jax.experimental.pallas import tpu as pltpu

f32 = jnp.float32
bf16 = jnp.bfloat16
i32 = jnp.int32

D_MODEL = 2048
GRID_W = 64
HEAD_DIM = 128
A_HEADS = 8
A_KV_HEADS = 2
A_GROUP = A_HEADS // A_KV_HEADS
NA_HEADS = 8
NA_WIN_ROWS = 8
NA_WIN_COLS = 16
Q_BLOCK = 128
ROPE_THETA = 10000.0
N_GROUPS = 4
EXPERTS_PER_GROUP = 8
N_EXPERTS = N_GROUPS * EXPERTS_PER_GROUP
D_EXPERT = D_MODEL // 4
EPS = 1e-6
NEG_INF = -1e30
SCALE = HEAD_DIM ** -0.5
LOG2E = 1.4426950408889634

QA0, KA0, VA0, QN0, KN0, VN0, GATE0 = 0, 8, 10, 12, 20, 28, 36
IN_COLS = (GATE0 + 2 * D_MODEL // HEAD_DIM) * HEAD_DIM

LANES = 128
NA_BAND_ROWS = 10
NA_KEYS = NA_BAND_ROWS * GRID_W
NA_PAIR_ROWS = NA_BAND_ROWS + Q_BLOCK // GRID_W
ROUTE_LANE0 = 4
EXPERT_BLOCK = 256
DMA_UNROLL = 8
DMA_GROUP = 32
VMEM_LIMIT = 56 * 1024 * 1024


def _cparams(sem):
    return pltpu.CompilerParams(dimension_semantics=sem, vmem_limit_bytes=VMEM_LIMIT)


def _inproj_body(x_ref, g_ref, w_ref, cos_ref, sin_ref, qg_ref, kg_ref, o_ref, h_ref):
    j = pl.program_id(1)

    @pl.when(j == 0)
    def _():
        x = x_ref[...]
        ms = jnp.mean(x * x, axis=-1, keepdims=True)
        h_ref[...] = (x * lax.rsqrt(ms + EPS) * g_ref[...]).astype(bf16)

    @pl.when(j != 0)
    def _():
        o_ref[...] = jnp.dot(h_ref[...], w_ref[...], preferred_element_type=f32).astype(o_ref.dtype)

    @pl.when(j == 0)
    def _():
        cos = cos_ref[...]
        sin = sin_ref[...]
        lane = lax.broadcasted_iota(i32, cos.shape, 1)
        first = (lane % (HEAD_DIM // 2)) < (HEAD_DIM // 4)

        def prep(x, g, scale):
            y = x * lax.rsqrt(jnp.mean(x * x, axis=-1, keepdims=True) + EPS) * g
            rot = jnp.where(first, pltpu.roll(y, HEAD_DIM - HEAD_DIM // 4, 1), pltpu.roll(y, HEAD_DIM // 4, 1))
            return (y * cos + rot * sin) * scale

        h_in = h_ref[...]
        for h0 in range(0, A_HEADS + A_KV_HEADS, 2):
            acc = jnp.dot(h_in, w_ref[:, h0 * HEAD_DIM:(h0 + 2) * HEAD_DIM], preferred_element_type=f32)
            for h in (h0, h0 + 1):
                is_q = h < A_HEADS
                seg = acc[:, (h - h0) * HEAD_DIM:(h - h0 + 1) * HEAD_DIM]
                o_ref[:, h * HEAD_DIM:(h + 1) * HEAD_DIM] = prep(
                    seg, (qg_ref if is_q else kg_ref)[...], SCALE * LOG2E if is_q else 1.0).astype(o_ref.dtype)
        rest = (A_HEADS + A_KV_HEADS) * HEAD_DIM
        o_ref[:, rest:] = jnp.dot(h_in, w_ref[:, rest:], preferred_element_type=f32).astype(o_ref.dtype)


def _inproj(x, g, w, cos, sin, qg, kg, seq, tm=1024, tn=IN_COLS // 4):
    n, d = x.shape
    c = w.shape[1]
    tm = min(tm, seq)
    spb = seq // tm
    assert QA0 == 0 and KA0 == A_HEADS and tn >= (A_HEADS + A_KV_HEADS) * HEAD_DIM
    return pl.pallas_call(
        _inproj_body,
        grid=(n // tm, c // tn),
        in_specs=[pl.BlockSpec((tm, d), lambda i, j: (i, 0)),
                  pl.BlockSpec((1, d), lambda i, j: (0, 0)),
                  pl.BlockSpec((d, tn), lambda i, j: (0, j)),
                  pl.BlockSpec((tm, HEAD_DIM), lambda i, j: (i % spb, 0)),
                  pl.BlockSpec((tm, HEAD_DIM), lambda i, j: (i % spb, 0)),
                  pl.BlockSpec((1, HEAD_DIM), lambda i, j: (0, 0)),
                  pl.BlockSpec((1, HEAD_DIM), lambda i, j: (0, 0))],
        out_specs=pl.BlockSpec((tm, tn), lambda i, j: (i, j)),
        out_shape=jax.ShapeDtypeStruct((n, c), bf16),
        scratch_shapes=[pltpu.VMEM((tm, d), bf16)],
        compiler_params=_cparams(("parallel", "arbitrary")),
        name="inproj",
    )(x, g, w, cos, sin, qg, kg)


def _lane_tile(x, n):
    return jnp.concatenate([x] * n, axis=1)


def _gqa_body(q_ref, k_ref, v_ref, o_ref, m_ref, acc_ref, *, tk):
    nkv = k_ref.shape[0] // tk
    m_ref[...] = jnp.full(m_ref.shape, NEG_INF, f32)
    acc_ref[...] = jnp.zeros(acc_ref.shape, f32)
    ones = jnp.ones((tk, HEAD_DIM), bf16)

    def step(j, c):
        off = pl.multiple_of(j * tk, tk)
        k = k_ref[pl.ds(off, tk), :]
        v1 = jnp.concatenate([v_ref[pl.ds(off, tk), :], ones], axis=1)

        def scores(g):
            q = q_ref[:, g * HEAD_DIM:(g + 1) * HEAD_DIM]
            return lax.dot_general(q, k, (((1,), (1,)), ((), ())), preferred_element_type=f32)

        s = scores(0)
        for g in range(A_GROUP):
            s_next = scores(g + 1) if g + 1 < A_GROUP else None
            m = m_ref[g]
            m_new = jnp.maximum(m, jnp.max(s, axis=-1, keepdims=True))
            alpha = jnp.exp2(m - m_new)
            p = jnp.exp2(s - _lane_tile(m_new, tk // LANES))
            pv = jnp.dot(p.astype(bf16), v1, preferred_element_type=f32)
            acc_ref[g] = _lane_tile(alpha, 2) * acc_ref[g] + pv
            m_ref[g] = m_new
            s = s_next
        return c

    lax.fori_loop(0, nkv, step, 0)
    for g in range(A_GROUP):
        acc = acc_ref[g]
        o_ref[:, g * HEAD_DIM:(g + 1) * HEAD_DIM] = (acc[:, :HEAD_DIM] / acc[:, HEAD_DIM:]).astype(o_ref.dtype)


def _gqa(proj, batch, seq, tq=1024, tk=2048):
    n = proj.shape[0]
    tq, tk = min(tq, seq), min(tk, seq)
    nq = seq // tq
    gw = A_GROUP * HEAD_DIM
    return pl.pallas_call(
        functools.partial(_gqa_body, tk=tk),
        grid=(batch, A_KV_HEADS, nq),
        in_specs=[pl.BlockSpec((tq, gw), lambda b, h, i: (b * nq + i, QA0 // A_GROUP + h)),
                  pl.BlockSpec((seq, HEAD_DIM), lambda b, h, i: (b, KA0 + h)),
                  pl.BlockSpec((seq, HEAD_DIM), lambda b, h, i: (b, VA0 + h))],
        out_specs=pl.BlockSpec((tq, gw), lambda b, h, i: (b * nq + i, h)),
        out_shape=jax.ShapeDtypeStruct((n, A_HEADS * HEAD_DIM), bf16),
        scratch_shapes=[pltpu.VMEM((A_GROUP, tq, LANES), f32), pltpu.VMEM((A_GROUP, tq, 2 * HEAD_DIM), f32)],
        compiler_params=_cparams(("parallel", "parallel", "arbitrary")),
        name="gqa",
    )(proj, proj, proj)


def _na_bias_tables(rpb, rows):
    nblk = rows * GRID_W // Q_BLOCK
    rpq = Q_BLOCK // GRID_W
    ncls = 5
    blk = jnp.array([0, 1, 2, nblk - 2, nblk - 1], i32)
    col = jnp.arange(GRID_W, dtype=i32)
    col_start = jnp.clip(col - NA_WIN_COLS // 2, 0, GRID_W - NA_WIN_COLS)
    col_in = (col[None, :] >= col_start[:, None]) & (col[None, :] < col_start[:, None] + NA_WIN_COLS)
    dc = jnp.clip(col[None, :] - col[:, None] + NA_WIN_COLS - 1, 0, 2 * NA_WIN_COLS - 2)
    r0 = blk * rpq
    bs = jnp.clip(r0 - NA_WIN_ROWS // 2, 0, rows - NA_BAND_ROWS)
    q_row = r0[:, None] + jnp.arange(rpq, dtype=i32)[None, :]
    k_row = bs[:, None] + jnp.arange(NA_BAND_ROWS, dtype=i32)[None, :]
    row_start = jnp.clip(q_row - NA_WIN_ROWS // 2, 0, rows - NA_WIN_ROWS)
    row_in = (k_row[:, None, :] >= row_start[:, :, None]) & (k_row[:, None, :] < row_start[:, :, None] + NA_WIN_ROWS)
    dr = jnp.clip(k_row[:, None, :] - q_row[:, :, None] + NA_WIN_ROWS - 1, 0, 2 * NA_WIN_ROWS - 2)
    picked = rpb.astype(f32)[:, dr, :]
    onehot = (dc[None] == jnp.arange(2 * NA_WIN_COLS - 1, dtype=i32)[:, None, None]).astype(f32)
    bias = jnp.einsum("hcqkd,dxy->chqxky", picked, onehot, precision=lax.Precision.HIGHEST)
    mask = row_in[:, None, :, None, :, None] & col_in[None, None, None, :, None, :]
    bias = jnp.where(mask, bias * LOG2E, NEG_INF)
    return bias.reshape(ncls, NA_HEADS, Q_BLOCK, NA_KEYS)


def _na_band_row(first_q_row, rows, band_rows):
    return jnp.clip(first_q_row - NA_WIN_ROWS // 2, 0, rows - band_rows)


def _na_body(q_ref, k_ref, v_ref, tab0_ref, tab1_ref, o_ref, *, rows):
    nh = q_ref.shape[1] // HEAD_DIM
    rpq = Q_BLOCK // GRID_W
    first_q_row = pl.program_id(1) * (2 * rpq)
    win_row = _na_band_row(first_q_row, rows, NA_PAIR_ROWS)
    tabs = (tab0_ref, tab1_ref)
    offs = [pl.multiple_of((_na_band_row(first_q_row + j * rpq, rows, NA_BAND_ROWS) - win_row) * GRID_W, GRID_W)
            for j in range(2)]
    ones = jnp.ones((NA_KEYS, HEAD_DIM), bf16)
    units = [(j, h) for j in range(2) for h in range(nh)]

    def scores(j, h):
        sl = slice(h * HEAD_DIM, (h + 1) * HEAD_DIM)
        q = q_ref[j * Q_BLOCK:(j + 1) * Q_BLOCK, sl]
        k = k_ref[pl.ds(offs[j], NA_KEYS), sl]
        s = lax.dot_general(q, k, (((1,), (1,)), ((), ())), preferred_element_type=f32)
        return s * (SCALE * LOG2E) + tabs[j][0, h]

    s = scores(*units[0])
    for u, (j, h) in enumerate(units):
        sl = slice(h * HEAD_DIM, (h + 1) * HEAD_DIM)
        s_next = scores(*units[u + 1]) if u + 1 < len(units) else None
        m = jnp.broadcast_to(jnp.max(s, axis=-1, keepdims=True), (Q_BLOCK, LANES))
        p = jnp.exp2(s - _lane_tile(m, NA_KEYS // LANES))
        v1 = jnp.concatenate([v_ref[pl.ds(offs[j], NA_KEYS), sl], ones], axis=1)
        pv = jnp.dot(p.astype(bf16), v1, preferred_element_type=f32)
        o_ref[j * Q_BLOCK:(j + 1) * Q_BLOCK, sl] = (pv[:, :HEAD_DIM] / pv[:, HEAD_DIM:]).astype(o_ref.dtype)
        s = s_next


def _na(proj, tab, batch, seq):
    n = proj.shape[0]
    nblk = seq // Q_BLOCK
    rows = seq // GRID_W
    rpq = Q_BLOCK // GRID_W
    w = NA_HEADS * HEAD_DIM
    assert nblk % 2 == 0 and rows >= NA_PAIR_ROWS
    npair = nblk // 2
    win_keys = NA_PAIR_ROWS * GRID_W

    def window(b, i):
        return pl.multiple_of(b * seq + _na_band_row(i * 2 * rpq, rows, NA_PAIR_ROWS) * GRID_W, GRID_W)

    def cls(blk):
        return jnp.where(blk < 2, blk, jnp.where(blk <= nblk - 3, 2, blk - (nblk - 5)))

    return pl.pallas_call(
        functools.partial(_na_body, rows=rows),
        grid=(batch, npair),
        in_specs=[pl.BlockSpec((pl.Element(2 * Q_BLOCK), pl.Element(w)),
                               lambda b, i: ((b * npair + i) * 2 * Q_BLOCK, QN0 * HEAD_DIM)),
                  pl.BlockSpec((pl.Element(win_keys), pl.Element(w)), lambda b, i: (window(b, i), KN0 * HEAD_DIM)),
                  pl.BlockSpec((pl.Element(win_keys), pl.Element(w)), lambda b, i: (window(b, i), VN0 * HEAD_DIM)),
                  pl.BlockSpec((1, NA_HEADS, Q_BLOCK, NA_KEYS), lambda b, i: (cls(2 * i), 0, 0, 0)),
                  pl.BlockSpec((1, NA_HEADS, Q_BLOCK, NA_KEYS), lambda b, i: (cls(2 * i + 1), 0, 0, 0))],
        out_specs=pl.BlockSpec((2 * Q_BLOCK, w), lambda b, i: (b * npair + i, 0)),
        out_shape=jax.ShapeDtypeStruct((n, w), bf16),
        compiler_params=_cparams(("parallel", "arbitrary")),
        name="natten",
    )(proj, proj, proj, tab, tab)


def _pack_bf16_pairs(x):
    c = x.shape[1] // 2
    bits = lax.bitcast_convert_type(x.astype(f32), i32)
    return lax.shift_right_logical(bits[:, :c], 16) | bits[:, c:]


def _unpack_bf16_pairs(packed):
    lo = lax.bitcast_convert_type(lax.shift_left(packed, 16), f32)
    hi = lax.bitcast_convert_type(packed & jnp.int32(-65536), f32)
    return lo, hi


def _merge_body(oa_ref, on_ref, ga_ref, gb_ref, x_ref, wa_ref, wb_ref, wo_ref, g2_ref, wr_ref, br_ref,
                x1_ref, h2_ref, ri_ref, rf_ref, cnt_ref, carry_ref):
    tm = x_ref.shape[0]

    @pl.when(pl.program_id(0) == 0)
    def _():
        carry_ref[...] = jnp.zeros_like(carry_ref)

    a = jnp.dot(oa_ref[...], wa_ref[...], preferred_element_type=f32)
    b = jnp.dot(on_ref[...], wb_ref[...], preferred_element_type=f32)
    ga = jax.nn.sigmoid(ga_ref[...].astype(f32))
    gb = jax.nn.sigmoid(gb_ref[...].astype(f32))
    merged = (ga * a + gb * b).astype(bf16)
    x1 = x_ref[...] + jnp.dot(merged, wo_ref[...], preferred_element_type=f32)
    x1_ref[...] = x1

    h2 = x1 * lax.rsqrt(jnp.mean(x1 * x1, axis=-1, keepdims=True) + EPS) * g2_ref[...]
    h_hi = h2.astype(bf16)
    h2_ref[...] = _pack_bf16_pairs(h_hi)
    h_lo = (h2 - h_hi.astype(f32)).astype(bf16)
    hw = jnp.dot(h_hi, wr_ref[...], preferred_element_type=f32)
    logits = (hw[:, :LANES] + hw[:, LANES:]
              + jnp.dot(h_lo, wr_ref[:, :LANES], preferred_element_type=f32)) + br_ref[...]

    lane = lax.broadcasted_iota(i32, logits.shape, 1)
    gl = jnp.where(lane < N_GROUPS, logits, NEG_INF)
    gmax = jnp.max(gl, axis=-1, keepdims=True)
    gsel = jnp.min(jnp.where(gl == gmax, lane, LANES), axis=-1, keepdims=True)
    gw = 1.0 / jnp.sum(jnp.exp(gl - gmax), axis=-1, keepdims=True)
    lo = ROUTE_LANE0 + gsel * EXPERTS_PER_GROUP
    el = jnp.where((lane >= lo) & (lane < lo + EXPERTS_PER_GROUP), logits, NEG_INF)
    m1 = jnp.max(el, axis=-1, keepdims=True)
    i1 = jnp.min(jnp.where(el == m1, lane, LANES), axis=-1, keepdims=True)
    el2 = jnp.where(lane == i1, NEG_INF, el)
    m2 = jnp.max(el2, axis=-1, keepdims=True)
    i2 = jnp.min(jnp.where(el2 == m2, lane, LANES), axis=-1, keepdims=True)
    r = jnp.exp(m2 - m1)
    gate1 = gw / (1.0 + r)
    gate2 = gw * r / (1.0 + r)

    oh = ((lane == i1) | (lane == i2)).astype(bf16)
    row = lax.broadcasted_iota(i32, (tm, tm), 0)
    col = lax.broadcasted_iota(i32, (tm, tm), 1)
    tri = (col < row).astype(bf16)
    before = carry_ref[...] + jnp.dot(tri, oh, preferred_element_type=f32)
    r1 = jnp.sum(jnp.where(lane == i1, before, 0.0), axis=-1, keepdims=True).astype(i32)
    r2 = jnp.sum(jnp.where(lane == i2, before, 0.0), axis=-1, keepdims=True).astype(i32)
    carry = carry_ref[...] + jnp.sum(oh.astype(f32), axis=0, keepdims=True)
    carry_ref[...] = carry
    cnt_ref[...] = carry

    ri_ref[...] = jnp.where(lane == 0, i1 - ROUTE_LANE0, jnp.where(lane == 1, i2 - ROUTE_LANE0,
                            jnp.where(lane == 2, r1, jnp.where(lane == 3, r2, 0))))
    rf_ref[...] = jnp.where(lane == 0, gate1, jnp.where(lane == 1, gate2, 0.0))


def _merge(oa, on, proj, x, wa, wb, wo, g2, wr, br, tm=256):
    n, d = x.shape
    tm = min(tm, n)
    aw = oa.shape[1]
    const = lambda shape: pl.BlockSpec(shape, lambda i: (0,) * len(shape), pipeline_mode=pl.Buffered(1))
    return pl.pallas_call(
        _merge_body,
        grid=(n // tm,),
        in_specs=[pl.BlockSpec((tm, aw), lambda i: (i, 0)),
                  pl.BlockSpec((tm, aw), lambda i: (i, 0)),
                  pl.BlockSpec((pl.Element(tm), pl.Element(d)), lambda i: (i * tm, GATE0 * HEAD_DIM)),
                  pl.BlockSpec((pl.Element(tm), pl.Element(d)), lambda i: (i * tm, GATE0 * HEAD_DIM + d)),
                  pl.BlockSpec((tm, d), lambda i: (i, 0)),
                  const((aw, d)), const((aw, d)), const((d, d)), const((1, d)),
                  const((d, 2 * LANES)), const((1, LANES))],
        out_specs=[pl.BlockSpec((tm, d), lambda i: (i, 0)),
                   pl.BlockSpec((tm, d // 2), lambda i: (i, 0)),
                   pl.BlockSpec((tm, LANES), lambda i: (i, 0)),
                   pl.BlockSpec((tm, LANES), lambda i: (i, 0)),
                   pl.BlockSpec((1, LANES), lambda i: (0, 0))],
        out_shape=[jax.ShapeDtypeStruct((n, d), f32), jax.ShapeDtypeStruct((n, d // 2), i32),
                   jax.ShapeDtypeStruct((n, LANES), i32), jax.ShapeDtypeStruct((n, LANES), f32),
                   jax.ShapeDtypeStruct((1, LANES), f32)],
        scratch_shapes=[pltpu.VMEM((1, LANES), f32)],
        compiler_params=_cparams(("arbitrary",)),
        name="merge_route",
    )(oa, on, proj, proj, x, wa, wb, wo, g2, wr, br)


def _moe_body(d1_ref, d2_ref, be_ref, bv_ref, nu_ref, h_hbm, wi_ref, wd_ref, y_hbm,
              xin0, xin1, yo0, yo1, wi_bf, wd_bf, enc_ref, gsem, ssem):
    b = pl.program_id(0)
    nu = nu_ref[0]
    n = d1_ref.shape[0]
    tb = xin0.shape[0]
    nb = be_ref.shape[0]
    xin, yo = (xin0, xin1), (yo0, yo1)

    def gather(blk, r, slot):
        enc = enc_ref[blk * tb + r]
        return pltpu.make_async_copy(h_hbm.at[pl.ds(enc & (n - 1), 1), :], xin[slot].at[pl.ds(r, 1), :],
                                     gsem.at[slot])

    def scatter(blk, r, slot):
        enc = enc_ref[blk * tb + r]
        return pltpu.make_async_copy(yo[slot].at[pl.ds(r, 1), :], y_hbm.at[pl.ds(enc, 1), :], ssem.at[slot])

    def each_row(blk, copy, wait=False):
        cnt = bv_ref[blk]
        for g in range(tb // DMA_GROUP):
            @pl.when((g + 1) * DMA_GROUP <= cnt)
            def _():
                for r in range(g * DMA_GROUP, (g + 1) * DMA_GROUP):
                    if wait:
                        copy(r).wait()
                    else:
                        copy(r).start(priority=r % 2)

        def single(r, c):
            if wait:
                copy(r).wait()
            else:
                copy(r).start()
            return c

        lax.fori_loop(cnt // DMA_GROUP * DMA_GROUP, cnt, single, 0)

    @pl.when(b == 0)
    def _():
        xin0[...] = jnp.zeros_like(xin0)
        xin1[...] = jnp.zeros_like(xin1)

        def fill(t, c):
            enc_ref[d1_ref[t]] = t
            enc_ref[d2_ref[t]] = n + t
            return c

        lax.fori_loop(0, n, fill, 0, unroll=DMA_UNROLL)
        each_row(0, lambda r: gather(0, r, 0))

    def step(p):
        nxt = jnp.minimum(b + 1, nb - 1)

        @pl.when(b >= 2)
        def _():
            each_row(b - 2, lambda r: scatter(b - 2, r, p), wait=True)

        each_row(b, lambda r: gather(b, r, p), wait=True)
        each_row(nxt, lambda r: gather(nxt, r, 1 - p))
        x_lo, x_hi = _unpack_bf16_pairs(xin[p][...])
        x = jnp.concatenate([x_lo.astype(bf16), x_hi.astype(bf16)], axis=1)
        h = jnp.dot(x, wi_bf[...], preferred_element_type=f32)
        a, u = h[:, :D_EXPERT], h[:, D_EXPERT:]
        act = (a * jax.nn.sigmoid(a) * u).astype(bf16)
        y = jnp.dot(act, wd_bf[...], preferred_element_type=f32)
        yo[p][...] = _pack_bf16_pairs(y.astype(bf16))
        each_row(b, lambda r: scatter(b, r, p))

        @pl.when(b == nu - 1)
        def _():
            each_row(nxt, lambda r: gather(nxt, r, 1 - p), wait=True)
            each_row(b, lambda r: scatter(b, r, p), wait=True)

            @pl.when(b >= 1)
            def _():
                each_row(b - 1, lambda r: scatter(b - 1, r, 1 - p), wait=True)

    @pl.when((b < nu) & ((b == 0) | (be_ref[b] != be_ref[jnp.maximum(b - 1, 0)])))
    def _():
        wi_bf[...] = wi_ref[0].astype(bf16)
        wd_bf[...] = wd_ref[0].astype(bf16)

    for p in range(2):
        pl.when((b < nu) & (b % 2 == p))(functools.partial(step, p))


def _moe(d1, d2, blk_expert, blk_valid, n_used, h2, wi, wd):
    n, dp = h2.shape
    d = 2 * dp
    assert n & (n - 1) == 0, "token count must be a power of two (row index = enc & (n - 1))"
    tb = EXPERT_BLOCK
    nb = blk_expert.shape[0]
    return pl.pallas_call(
        _moe_body,
        grid_spec=pltpu.PrefetchScalarGridSpec(
            num_scalar_prefetch=5,
            grid=(nb,),
            in_specs=[pl.BlockSpec(memory_space=pl.ANY),
                      pl.BlockSpec((1, d, 2 * D_EXPERT), lambda b, d1, d2, be, bv, nu: (be[b], 0, 0)),
                      pl.BlockSpec((1, D_EXPERT, d), lambda b, d1, d2, be, bv, nu: (be[b], 0, 0))],
            out_specs=pl.BlockSpec(memory_space=pl.ANY),
            scratch_shapes=[pltpu.VMEM((tb, dp), i32), pltpu.VMEM((tb, dp), i32),
                            pltpu.VMEM((tb, dp), i32), pltpu.VMEM((tb, dp), i32),
                            pltpu.VMEM((d, 2 * D_EXPERT), bf16), pltpu.VMEM((D_EXPERT, d), bf16),
                            pltpu.SMEM((nb * tb,), i32),
                            pltpu.SemaphoreType.DMA((2,)), pltpu.SemaphoreType.DMA((2,))]),
        out_shape=jax.ShapeDtypeStruct((2 * n, dp), i32),
        compiler_params=_cparams(("arbitrary",)),
        name="moe_experts",
    )(d1, d2, blk_expert, blk_valid, n_used, h2, wi, wd)


def _final_body(y1_ref, y2_ref, x1_ref, rf_ref, gf_ref, o_ref):
    rf = rf_ref[...]
    y1 = jnp.concatenate(_unpack_bf16_pairs(y1_ref[...]), axis=1)
    y2 = jnp.concatenate(_unpack_bf16_pairs(y2_ref[...]), axis=1)
    x = x1_ref[...] + (y1 * rf[:, 0:1] + y2 * rf[:, 1:2])
    o_ref[...] = x * lax.rsqrt(jnp.mean(x * x, axis=-1, keepdims=True) + EPS) * gf_ref[...]


def _final(y, x1, rf, gf, tm=512):
    n, d = x1.shape
    tm = min(tm, n)
    return pl.pallas_call(
        _final_body,
        grid=(n // tm,),
        in_specs=[pl.BlockSpec((tm, d // 2), lambda i: (i, 0)),
                  pl.BlockSpec((tm, d // 2), lambda i: (i + n // tm, 0)),
                  pl.BlockSpec((tm, d), lambda i: (i, 0)),
                  pl.BlockSpec((tm, LANES), lambda i: (i, 0)),
                  pl.BlockSpec((1, d), lambda i: (0, 0))],
        out_specs=pl.BlockSpec((tm, d), lambda i: (i, 0)),
        out_shape=jax.ShapeDtypeStruct((n, d), f32),
        compiler_params=_cparams(("parallel",)),
        name="moe_final",
    )(y, y, x1, rf, gf)


def _rope_tables(seq):
    rows = seq // GRID_W
    half = HEAD_DIM // 2
    inv_freq = ROPE_THETA ** (-jnp.arange(0, half, 2, dtype=f32) / half)
    inv_freq = jnp.concatenate([inv_freq, inv_freq])
    sign = jnp.where(jnp.arange(half) < half // 2, -1.0, 1.0).astype(f32)

    def table(fn, fold):
        by_row = fn(jnp.arange(rows, dtype=f32)[:, None] * inv_freq[None, :]) * fold
        by_col = fn(jnp.arange(GRID_W, dtype=f32)[:, None] * inv_freq[None, :]) * fold
        full = jnp.concatenate([jnp.broadcast_to(by_row[:, None, :], (rows, GRID_W, half)),
                                jnp.broadcast_to(by_col[None, :, :], (rows, GRID_W, half))], axis=-1)
        return full.reshape(seq, HEAD_DIM)

    return table(jnp.cos, 1.0), table(jnp.sin, sign)


def _layer(x, p):
    batch, seq, d = x.shape
    n = batch * seq
    x2 = x.reshape(n, d)
    assert seq // GRID_W >= NA_BAND_ROWS and seq // Q_BLOCK >= 5, "sequence too short for the 5 block geometries"
    proj = _inproj(x2, p["norm1_g"], p["w_in"], p["rope_cos"], p["rope_sin"], p["q_norm_g"], p["k_norm_g"], seq)
    oa = _gqa(proj, batch, seq)
    on = _na(proj, p["na_bias"], batch, seq)
    x1, h2, ri, rf, cnt = _merge(oa, on, proj, x2, p["w_branch_a"], p["w_branch_b"], p["w_out"], p["norm2_g"],
                                 p["w_router"], p["b_router"])

    tb = EXPERT_BLOCK
    n_rows = 2 * n + N_EXPERTS * tb
    counts = cnt[0, ROUTE_LANE0:ROUTE_LANE0 + N_EXPERTS].astype(i32)
    padded = (counts + tb - 1) // tb * tb
    pad_end = jnp.cumsum(padded)
    pad_start = pad_end - padded
    eids = jnp.arange(N_EXPERTS, dtype=i32)

    def start_of(e):
        return jnp.sum(jnp.where(e[:, None] == eids[None, :], pad_start[None, :], 0), axis=1)

    d1 = start_of(ri[:, 0]) + ri[:, 2]
    d2 = start_of(ri[:, 1]) + ri[:, 3]
    nb = n_rows // tb
    blk_start = jnp.arange(nb, dtype=i32) * tb
    blk_expert = jnp.minimum(jnp.sum((pad_end[None, :] <= blk_start[:, None]).astype(i32), axis=1), N_EXPERTS - 1)
    of_blk = blk_expert[:, None] == eids[None, :]
    row_in_expert = blk_start - jnp.sum(jnp.where(of_blk, pad_start[None, :], 0), axis=1)
    blk_valid = jnp.clip(jnp.sum(jnp.where(of_blk, counts[None, :], 0), axis=1) - row_in_expert, 0, tb)
    blk_valid = jnp.where(blk_start < pad_end[-1], blk_valid, 0).astype(i32)
    n_used = (pad_end[-1:] // tb).astype(i32)

    y = _moe(d1, d2, blk_expert, blk_valid, n_used, h2, p["w_exp_in"], p["w_exp_down"])
    out = _final(y, x1, rf, p["norm_f_g"])
    return out.reshape(batch, seq, d)


def kernel(x_prompt, x_sample, norm1_g, w_in, q_norm_g, k_norm_g, na_rpb, w_branch_a, w_branch_b, w_out, norm2_g,
           w_router_group, b_router_group, w_router_expert, b_router_expert, w_exp_in, w_exp_down, norm_f_g):
    assert norm1_g.shape[0] == 1, "one encoder layer"
    w_r = jnp.zeros((D_MODEL, LANES), f32)
    w_r = w_r.at[:, :N_GROUPS].set(w_router_group[0]).at[:, ROUTE_LANE0:ROUTE_LANE0 + N_EXPERTS].set(w_router_expert[0])
    b_r = jnp.zeros((1, LANES), f32)
    b_r = b_r.at[0, :N_GROUPS].set(b_router_group[0]).at[0, ROUTE_LANE0:ROUTE_LANE0 + N_EXPERTS].set(b_router_expert[0])
    w_r_hi = w_r.astype(bf16)
    p = {
        "norm1_g": norm1_g[0][None], "w_in": w_in[0].astype(bf16),
        "q_norm_g": q_norm_g[0][None], "k_norm_g": k_norm_g[0][None],
        "w_branch_a": w_branch_a[0].astype(bf16), "w_branch_b": w_branch_b[0].astype(bf16),
        "w_out": w_out[0].astype(bf16), "norm2_g": norm2_g[0][None],
        "w_router": jnp.concatenate([w_r_hi, (w_r - w_r_hi.astype(f32)).astype(bf16)], axis=1), "b_router": b_r,
        "w_exp_in": w_exp_in[0], "w_exp_down": w_exp_down[0],
        "norm_f_g": norm_f_g[None],
    }
    max_seq = max(x_prompt.shape[1], x_sample.shape[1])
    p["rope_cos"], p["rope_sin"] = _rope_tables(max_seq)
    p["na_bias"] = _na_bias_tables(na_rpb[0], max_seq // GRID_W)
    return _layer(x_prompt, p), _layer(x_sample, p)
```
